```python
import jax
import jax.numpy as jnp
from jax import lax
import numpy as np

D_MODEL = 1024
BATCH = 2
SEQ = 8192
DEPTH = 4
DEC_BATCH = 128
DEC_SEQ = 4
PAST_LEN = 2048
PAGE_SIZE = 128

N_A = DEPTH // 2
N_B = DEPTH - N_A
PLE_DIM = 256
D_FF = ((8 * D_MODEL // 3 + 127) // 128) * 128
RET_HEADS = 4
RET_DK = D_MODEL // RET_HEADS
RET_DV = 2 * RET_DK
RET_QK = RET_HEADS * RET_DK
RET_VDIM = RET_HEADS * RET_DV
RET_CHUNK = 128
ROPE_BASE = 10000.0
SB_HEADS = 16
SB_HD = D_MODEL // SB_HEADS
SB_BIAS_INIT = -8.0
Q_BLOCK = 128
EPS = 1e-6

kernel_name = 'yoco_retnet_stickbreaking_step'


def rmsnorm(x, g):
    xf = x.astype(jnp.float32)
    y = xf * lax.rsqrt(jnp.mean(xf * xf, axis=-1, keepdims=True) + EPS)
    return (y * g.astype(jnp.float32)).astype(x.dtype)


def swiglu(h, w_gate, w_up, w_down):
    return (jax.nn.silu(h @ w_gate) * (h @ w_up)) @ w_down


def rope(x, pos):
    half = x.shape[-1] // 2
    inv_freq = ROPE_BASE ** (-jnp.arange(half, dtype=jnp.float32) / half)
    ang = pos.astype(jnp.float32)[:, None] * inv_freq[None, :]
    cos = jnp.cos(ang)[None, :, None, :]
    sin = jnp.sin(ang)[None, :, None, :]
    xf = x.astype(jnp.float32)
    x1, x2 = xf[..., :half], xf[..., half:]
    return jnp.concatenate([x1 * cos - x2 * sin, x2 * cos + x1 * sin], axis=-1)


def retention_log_decay():
    return jnp.log(1.0 - 2.0 ** (-5.0 - jnp.arange(RET_HEADS, dtype=jnp.float32)))


def retention_chunk(S, q, k, v, lg):
    C = q.shape[1]
    idx = jnp.arange(C, dtype=jnp.float32)
    diff = idx[:, None] - idx[None, :]
    decay = jnp.where(diff[None] >= 0, jnp.exp(jnp.maximum(diff, 0.0)[None] * lg[:, None, None]), 0.0)
    s = jnp.einsum('bihd,bjhd->bhij', q, k) * decay[None]
    o = jnp.einsum('bhij,bjhv->bihv', s, v)
    q_dec = q * jnp.exp((idx + 1.0)[:, None] * lg[None, :])[None, :, :, None]
    o = o + jnp.einsum('bihd,bhdv->bihv', q_dec, S)
    k_dec = k * jnp.exp((C - 1.0 - idx)[:, None] * lg[None, :])[None, :, :, None]
    S_new = jnp.exp(C * lg)[None, :, None, None] * S + jnp.einsum('bjhd,bjhv->bhdv', k_dec, v)
    return o, S_new


def retention(q, k, v, S0, lg):
    B, T = q.shape[0], q.shape[1]
    if T <= RET_CHUNK or T % RET_CHUNK != 0:
        return retention_chunk(S0, q, k, v, lg)
    n = T // RET_CHUNK

    def to_chunks(a):
        return jnp.moveaxis(a.reshape(B, n, RET_CHUNK, *a.shape[2:]), 1, 0)

    def step(S, inp):
        o, S = retention_chunk(S, inp[0], inp[1], inp[2], lg)
        return S, o

    S_fin, o = lax.scan(step, S0, (to_chunks(q), to_chunks(k), to_chunks(v)))
    o = jnp.moveaxis(o, 0, 1).reshape(B, T, RET_HEADS, RET_DV)
    return o, S_fin


def retention_mixer(h, pos, S0, w_in, w_out, g_gn):
    B, T, _ = h.shape
    proj = h @ w_in
    q, k, v, g = jnp.split(proj, [RET_QK, 2 * RET_QK, 2 * RET_QK + RET_VDIM], axis=-1)
    q = rope(q.reshape(B, T, RET_HEADS, RET_DK), pos)
    k = rope(k.reshape(B, T, RET_HEADS, RET_DK), pos) * (RET_DK ** -0.5)
    v = v.reshape(B, T, RET_HEADS, RET_DV).astype(jnp.float32)
    o, S = retention(q, k, v, S0.astype(jnp.float32), retention_log_decay())
    mu = jnp.mean(o, axis=-1, keepdims=True)
    var = jnp.mean(jnp.square(o - mu), axis=-1, keepdims=True)
    o = (o - mu) * lax.rsqrt(var + EPS) * g_gn.astype(jnp.float32).reshape(RET_HEADS, RET_DV)
    o = (jax.nn.silu(g.astype(jnp.float32)) * o.reshape(B, T, RET_VDIM)).astype(h.dtype)
    return o @ w_out, S


def sb_block(q, k, v, q_pos, k_pos, bias):
    z = jnp.einsum('bqhd,bkhd->bhqk', q, k).astype(jnp.float32) * (SB_HD ** -0.5)
    z = z + bias.astype(jnp.float32)[None, :, None, None]
    mask = (k_pos[None, :] < q_pos[:, None])[None, None]
    log_1m = jnp.where(mask, jax.nn.log_sigmoid(-z), 0.0)
    later = lax.cumsum(log_1m, axis=3, reverse=True) - log_1m
    A = jnp.where(mask, jnp.exp(jax.nn.log_sigmoid(z) + later), 0.0)
    return jnp.einsum('bhqk,bkhd->bqhd', A.astype(v.dtype), v)


def stick_breaking(q, k, v, q_pos, k_pos, bias):
    B, T = q.shape[0], q.shape[1]
    if T <= Q_BLOCK or T % Q_BLOCK != 0:
        return sb_block(q, k, v, q_pos, k_pos, bias)
    n = T // Q_BLOCK
    qb = jnp.moveaxis(q.reshape(B, n, Q_BLOCK, SB_HEADS, SB_HD), 1, 0)
    pb = q_pos.reshape(n, Q_BLOCK)
    o = lax.map(lambda a: sb_block(a[0], k, v, a[1], k_pos, bias), (qb, pb))
    return jnp.moveaxis(o, 0, 1).reshape(B, T, SB_HEADS, SB_HD)


def sb_mixer(h, pos, keys, vals, key_pos, w_q, w_o, bias):
    B, T, _ = h.shape
    q = (h @ w_q).reshape(B, T, SB_HEADS, SB_HD)
    o = stick_breaking(q, keys, vals, pos, key_pos, bias)
    return o.reshape(B, T, SB_HEADS * SB_HD) @ w_o


def trunk(x, p, pos, ret_state, past_k, past_v, past_pos,
          g_norm, w_ff1_gate, w_ff1_up, w_ff1_down, w_ff2_gate, w_ff2_up, w_ff2_down,
          w_ple_up, w_ple_gate, g_ple, w_ret_in, w_ret_out, g_ret_gn,
          g_kv, w_kv, w_sb_q, w_sb_out, b_sb, g_final):
    B, T, _ = x.shape
    new_states = []
    k_new = v_new = None
    keys = vals = key_pos = None
    for i in range(DEPTH):
        x = x + 0.5 * swiglu(rmsnorm(x, g_norm[i, 0]), w_ff1_gate[i], w_ff1_up[i], w_ff1_down[i])
        h = rmsnorm(x, g_norm[i, 1])
        if i < N_A:
            o, S = retention_mixer(h, pos, ret_state[i], w_ret_in[i], w_ret_out[i], g_ret_gn[i])
            new_states.append(S)
        else:
            o = sb_mixer(h, pos, keys, vals, key_pos, w_sb_q[i - N_A], w_sb_out[i - N_A], b_sb[i - N_A])
        x = x + o
        x = x + 0.5 * swiglu(rmsnorm(x, g_norm[i, 2]), w_ff2_gate[i], w_ff2_up[i], w_ff2_down[i])
        x = x + (p[i] @ w_ple_up[i]) * jax.nn.sigmoid(rmsnorm(x, g_ple[i]) @ w_ple_gate[i])
        if i == N_A - 1:
            kv = rmsnorm(x, g_kv) @ w_kv
            k_new, v_new = jnp.split(kv, 2, axis=-1)
            k_new = k_new.reshape(B, T, SB_HEADS, SB_HD)
            v_new = v_new.reshape(B, T, SB_HEADS, SB_HD)
            if past_k is None:
                keys, vals, key_pos = k_new, v_new, pos
            else:
                keys = jnp.concatenate([past_k.astype(k_new.dtype), k_new], axis=1)
                vals = jnp.concatenate([past_v.astype(v_new.dtype), v_new], axis=1)
                key_pos = jnp.concatenate([past_pos, pos], axis=0)
    y = rmsnorm(x, g_final)
    return y, jnp.stack(new_states), k_new, v_new


def setup_inputs(seed: int = 0) -> dict:
    key = jax.random.key(seed)
    ks = jax.random.split(key, 32)
    n_pages = PAST_LEN // PAGE_SIZE
    n_used = DEC_BATCH * n_pages
    n_pool = n_used + max(1, n_used // 4)

    def nrm(k, shape, scale):
        return jax.random.normal(k, shape, jnp.float32) * scale

    page_table = jax.random.permutation(ks[7], n_pool)[:n_used].reshape(DEC_BATCH, n_pages).astype(jnp.int32)
    ret_in_width = 2 * RET_QK + 2 * RET_VDIM
    sb_w = SB_HEADS * SB_HD
    return {
        'x_prompt': nrm(ks[0], (BATCH, SEQ, D_MODEL), 1.0),
        'x_sample': nrm(ks[1], (DEC_BATCH, DEC_SEQ, D_MODEL), 1.0),
        'p_prompt': nrm(ks[2], (DEPTH, BATCH, SEQ, PLE_DIM), 1.0),
        'p_sample': nrm(ks[3], (DEPTH, DEC_BATCH, DEC_SEQ, PLE_DIM), 1.0),
        'state_ret': nrm(ks[4], (N_A, DEC_BATCH, RET_HEADS, RET_DK, RET_DV), 0.5),
        'cache_k': nrm(ks[5], (n_pool, PAGE_SIZE, SB_HEADS, SB_HD), 1.0),
        'cache_v': nrm(ks[6], (n_pool, PAGE_SIZE, SB_HEADS, SB_HD), 1.0),
        'page_table': page_table,
        'g_norm': 1.0 + nrm(ks[8], (DEPTH, 3, D_MODEL), 0.02),
        'w_ff1_gate': nrm(ks[9], (DEPTH, D_MODEL, D_FF), D_MODEL ** -0.5),
        'w_ff1_up': nrm(ks[10], (DEPTH, D_MODEL, D_FF), D_MODEL ** -0.5),
        'w_ff1_down': nrm(ks[11], (DEPTH, D_FF, D_MODEL), D_FF ** -0.5),
        'w_ff2_gate': nrm(ks[12], (DEPTH, D_MODEL, D_FF), D_MODEL ** -0.5),
        'w_ff2_up': nrm(ks[13], (DEPTH, D_MODEL, D_FF), D_MODEL ** -0.5),
        'w_ff2_down': nrm(ks[14], (DEPTH, D_FF, D_MODEL), D_FF ** -0.5),
        'w_ple_up': nrm(ks[15], (DEPTH, PLE_DIM, D_MODEL), PLE_DIM ** -0.5),
        'w_ple_gate': nrm(ks[16], (DEPTH, D_MODEL, D_MODEL), D_MODEL ** -0.5),
        'g_ple': 1.0 + nrm(ks[17], (DEPTH, D_MODEL), 0.02),
        'w_ret_in': nrm(ks[18], (N_A, D_MODEL, ret_in_width), D_MODEL ** -0.5),
        'w_ret_out': nrm(ks[19], (N_A, RET_VDIM, D_MODEL), RET_VDIM ** -0.5),
        'g_ret_gn': 1.0 + nrm(ks[20], (N_A, RET_VDIM), 0.02),
        'g_kv': 1.0 + nrm(ks[21], (D_MODEL,), 0.02),
        'w_kv': nrm(ks[22], (D_MODEL, 2 * sb_w), D_MODEL ** -0.5),
        'w_sb_q': nrm(ks[23], (N_B, D_MODEL, sb_w), D_MODEL ** -0.5),
        'w_sb_out': nrm(ks[24], (N_B, sb_w, D_MODEL), sb_w ** -0.5),
        'b_sb': SB_BIAS_INIT + nrm(ks[26], (N_B, SB_HEADS), 0.5),
        'g_final': 1.0 + nrm(ks[25], (D_MODEL,), 0.02),
    }


def reference(x_prompt, x_sample, p_prompt, p_sample, state_ret, cache_k, cache_v, page_table,
              g_norm, w_ff1_gate, w_ff1_up, w_ff1_down, w_ff2_gate, w_ff2_up, w_ff2_down,
              w_ple_up, w_ple_gate, g_ple, w_ret_in, w_ret_out, g_ret_gn,
              g_kv, w_kv, w_sb_q, w_sb_out, b_sb, g_final):
    weights = (g_norm, w_ff1_gate, w_ff1_up, w_ff1_down, w_ff2_gate, w_ff2_up, w_ff2_down,
               w_ple_up, w_ple_gate, g_ple, w_ret_in, w_ret_out, g_ret_gn,
               g_kv, w_kv, w_sb_q, w_sb_out, b_sb, g_final)
    n_prompt, t_prompt = x_prompt.shape[0], x_prompt.shape[1]
    ret_zero = jnp.zeros((N_A, n_prompt, RET_HEADS, RET_DK, RET_DV), jnp.float32)
    pos_p = jnp.arange(t_prompt, dtype=jnp.int32)
    y_p, S_p, k_p, v_p = trunk(x_prompt, p_prompt, pos_p, ret_zero, None, None, None, *weights)
    n_dec, t_dec = x_sample.shape[0], x_sample.shape[1]
    n_pages = page_table.shape[1]
    past_len = n_pages * PAGE_SIZE
    past_k = cache_k[page_table].reshape(n_dec, past_len, SB_HEADS, SB_HD)
    past_v = cache_v[page_table].reshape(n_dec, past_len, SB_HEADS, SB_HD)
    past_pos = jnp.arange(past_len, dtype=jnp.int32)
    pos_s = past_len + jnp.arange(t_dec, dtype=jnp.int32)
    y_s, S_s, k_s, v_s = trunk(x_sample, p_sample, pos_s, state_ret, past_k, past_v, past_pos, *weights)
    return (y_p, y_s, S_p.astype(x_prompt.dtype), S_s.astype(state_ret.dtype), k_p, v_p, k_s, v_s)
```

```python
import functools
import math

import jax
import jax.numpy as jnp
from jax import lax
from jax.experimental import pallas as pl
from jax.experimental.pallas import tpu as pltpu

F32 = jnp.float32
BF16 = jnp.bfloat16

EPS = 1e-6
ROPE_BASE = 10000.0
RET_HEADS = 4
SB_HEADS = 16
LANES = 128
VMEM_LIMIT_BYTES = 58 * 1024 * 1024
MASKED_LOGIT = -1e30
TOKEN_TILE = 512
RET_CHUNK_ROWS = 256
SB_BLOCK = 256


def _params(*sem):
    return pltpu.CompilerParams(dimension_semantics=sem, vmem_limit_bytes=VMEM_LIMIT_BYTES)


def _dot(a, b):
    return jnp.dot(a, b, preferred_element_type=F32)


def _dot_nt(a, b):
    return lax.dot_general(a, b, (((1,), (1,)), ((), ())), preferred_element_type=F32)


def _rmsnorm(x, g):
    y = x * lax.rsqrt(jnp.mean(x * x, axis=-1, keepdims=True) + EPS)
    return y * g


def _silu(x):
    return x * jax.nn.sigmoid(x)


def _softplus(z):
    return jnp.maximum(z, 0.0) + jnp.log(1.0 + jnp.exp(-jnp.abs(z)))


def _split_bf16(x):
    hi = x.astype(BF16)
    lo = (x - hi.astype(F32)).astype(BF16)
    return hi, lo


def _resident(arr):
    nd = arr.ndim
    return pl.BlockSpec(arr.shape, lambda *_: (0,) * nd, pipeline_mode=pl.Buffered(1))


def _tokenwise(body, streams, consts, outs, name):
    n_p, n_s = streams[0][0].shape[0], streams[0][1].shape[0]
    tm = min(TOKEN_TILE, n_s)
    assert n_p % tm == 0 and n_s % tm == 0
    p_tiles, s_tiles = n_p // tm, n_s // tm
    ns, nc, no = len(streams), len(consts), len(outs)

    def kern(*refs):
        in_refs, c_refs, o_refs = refs[:2 * ns], refs[2 * ns:2 * ns + nc], refs[2 * ns + nc:]
        is_prompt = pl.program_id(0) < p_tiles
        vals = [jnp.where(is_prompt, in_refs[2 * k][...], in_refs[2 * k + 1][...]) for k in range(ns)]
        res = body(*vals, *c_refs)

        @pl.when(is_prompt)
        def _():
            for k in range(no):
                o_refs[2 * k][...] = res[k].astype(outs[k][1])

        @pl.when(jnp.logical_not(is_prompt))
        def _():
            for k in range(no):
                o_refs[2 * k + 1][...] = res[k].astype(outs[k][1])

    def p_spec(w):
        return pl.BlockSpec((tm, w), lambda i: (jnp.minimum(i, p_tiles - 1), 0))

    def s_spec(w):
        return pl.BlockSpec((tm, w), lambda i: (jnp.maximum(i - p_tiles, 0), 0))

    in_specs, args = [], []
    for a_p, a_s in streams:
        in_specs += [p_spec(a_p.shape[1]), s_spec(a_s.shape[1])]
        args += [a_p, a_s]
    in_specs += [_resident(c) for c in consts]
    out_specs, out_shape = [], []
    for w, dt in outs:
        out_specs += [p_spec(w), s_spec(w)]
        out_shape += [jax.ShapeDtypeStruct((n_p, w), dt), jax.ShapeDtypeStruct((n_s, w), dt)]
    res = pl.pallas_call(
        kern, grid=(p_tiles + s_tiles,), in_specs=in_specs, out_specs=out_specs,
        out_shape=out_shape, compiler_params=_params("arbitrary"), name=name,
    )(*args, *consts)
    return [(res[2 * k], res[2 * k + 1]) for k in range(no)]


def _rowwise(body, rows, consts, outs, name):
    n = rows[0].shape[0]
    tm = min(TOKEN_TILE, n)
    assert n % tm == 0
    nr, nc, no = len(rows), len(consts), len(outs)

    def kern(*refs):
        res = body(*[r[...] for r in refs[:nr]], *refs[nr:nr + nc])
        for k in range(no):
            refs[nr + nc + k][...] = res[k].astype(outs[k][1])

    in_specs = [pl.BlockSpec((tm, a.shape[1]), lambda i: (i, 0)) for a in rows]
    in_specs += [_resident(c) for c in consts]
    return pl.pallas_call(
        kern, grid=(n // tm,), in_specs=in_specs,
        out_specs=[pl.BlockSpec((tm, w), lambda i: (i, 0)) for w, _ in outs],
        out_shape=[jax.ShapeDtypeStruct((n, w), dt) for w, dt in outs],
        compiler_params=_params("arbitrary"), name=name,
    )(*rows, *consts)


def _ffn_body(x, g_ref, wg_ref, wu_ref, wd_ref):
    h = _rmsnorm(x, g_ref[...]).astype(BF16)
    d_ff = wg_ref.shape[1]
    chunk = 256 if d_ff % 256 == 0 else LANES
    acc = None
    for c0 in range(0, d_ff, chunk):
        gate = _dot(h, wg_ref[:, c0:c0 + chunk])
        up = _dot(h, wu_ref[:, c0:c0 + chunk])
        d = _dot((_silu(gate) * up).astype(BF16), wd_ref[c0:c0 + chunk, :])
        acc = d if acc is None else acc + d
    return (x + 0.5 * acc,)


def _ple_body(x, p, g_ref, wup_ref, wgate_ref):
    h = _rmsnorm(x, g_ref[...]).astype(BF16)
    gate = jax.nn.sigmoid(_dot(h, wgate_ref[...]))
    return (x + _dot(p.astype(BF16), wup_ref[...]) * gate,)


def _norm_proj_body(x, g_ref, w_ref):
    return (_dot(_rmsnorm(x, g_ref[...]).astype(BF16), w_ref[...]),)


def _kv_body(x, g_ref, w_ref):
    h = _rmsnorm(x, g_ref[...]).astype(BF16)
    half = w_ref.shape[1] // 2
    k = _dot(h, w_ref[:, :half])
    v = _dot(h, w_ref[:, half:])
    return (k, v, k, v)


def _proj_residual_body(x, a, w_ref):
    return (x + _dot(a.astype(BF16), w_ref[...]),)


def _final_norm_body(x, g_ref):
    return (_rmsnorm(x, g_ref[...]),)


def _ret_in_proj_body(x, g_ref, w_ref, *, qk, vd):
    h = _rmsnorm(x, g_ref[...]).astype(BF16)
    return (_dot(h, w_ref[:, :qk]), _dot(h, w_ref[:, qk:2 * qk]),
            _dot(h, w_ref[:, 2 * qk:2 * qk + vd]), _dot(h, w_ref[:, 2 * qk + vd:]))


def _rope_tables(pos, half):
    inv_freq = ROPE_BASE ** (-jnp.arange(half, dtype=F32) / half)
    ang = pos.astype(F32)[:, None] * inv_freq[None, :]
    return jnp.cos(ang), jnp.sin(ang)


def _rope(x, cos, sin):
    half = x.shape[-1] // 2
    x1, x2 = x[:, :half], x[:, half:]
    return jnp.concatenate([x1 * cos - x2 * sin, x2 * cos + x1 * sin], axis=-1)


def _log_decay(head):
    return math.log(1.0 - 2.0 ** (-5.0 - head))


def _group_norm_gate(o, g, ggn):
    mu = jnp.mean(o, axis=-1, keepdims=True)
    var = jnp.mean(jnp.square(o - mu), axis=-1, keepdims=True)
    on = (o - mu) * lax.rsqrt(var + EPS) * ggn
    return _silu(g) * on


def _ret_prompt_kernel(x_ref, gn_ref, win_ref, wout_ref, ggn_ref, cos_ref, sin_ref,
                       o_ref, s_ref, *, rows, dk, dv):
    qk, vd = RET_HEADS * dk, RET_HEADS * dv

    @pl.when(pl.program_id(1) == 0)
    def _():
        s_ref[...] = jnp.zeros_like(s_ref)

    x = x_ref[...]
    h = _rmsnorm(x, gn_ref[...]).astype(BF16)
    cos, sin = cos_ref[...], sin_ref[...]
    diff = (lax.broadcasted_iota(jnp.int32, (rows, rows), 0)
            - lax.broadcasted_iota(jnp.int32, (rows, rows), 1)).astype(F32)
    ri = lax.broadcasted_iota(jnp.int32, (rows, 1), 0).astype(F32)
    acc = None
    for hh in range(RET_HEADS):
        lg = _log_decay(hh)
        q = _rope(_dot(h, win_ref[:, hh * dk:(hh + 1) * dk]), cos, sin)
        k = _rope(_dot(h, win_ref[:, qk + hh * dk:qk + (hh + 1) * dk]), cos, sin) * (dk ** -0.5)
        v = _dot(h, win_ref[:, 2 * qk + hh * dv:2 * qk + (hh + 1) * dv]).astype(BF16)
        g = _dot(h, win_ref[:, 2 * qk + vd + hh * dv:2 * qk + vd + (hh + 1) * dv])
        decay = jnp.where(diff >= 0, jnp.exp(jnp.maximum(diff, 0.0) * lg), 0.0)
        s = _dot_nt(q.astype(BF16), k.astype(BF16)) * decay
        state = s_ref[0, hh]
        q_dec = (q * jnp.exp((ri + 1.0) * lg)).astype(BF16)
        o = _dot(s.astype(BF16), v) + _dot(q_dec, state.astype(BF16))
        k_dec = k * jnp.exp((rows - 1.0 - ri) * lg)
        s_ref[0, hh] = math.exp(rows * lg) * state + _dot(k_dec.T.astype(BF16), v)
        gated = _group_norm_gate(o, g, ggn_ref[:, hh * dv:(hh + 1) * dv]).astype(BF16)
        d = _dot(gated, wout_ref[hh * dv:(hh + 1) * dv, :])
        acc = d if acc is None else acc + d
    o_ref[...] = x + acc


def _retention_prompt(xp, batch, g, w_in, w_out, g_gn, dk, dv):
    n_p, d = xp.shape
    t = n_p // batch
    rows = min(RET_CHUNK_ROWS, t)
    assert t % rows == 0
    nch = t // rows
    cos, sin = _rope_tables(jnp.arange(t, dtype=jnp.int32), dk // 2)
    kern = functools.partial(_ret_prompt_kernel, rows=rows, dk=dk, dv=dv)
    return pl.pallas_call(
        kern, grid=(batch, nch),
        in_specs=[pl.BlockSpec((rows, d), lambda b, c: (b * nch + c, 0)),
                  _resident(g), _resident(w_in), _resident(w_out), _resident(g_gn),
                  pl.BlockSpec((rows, dk // 2), lambda b, c: (c, 0)),
                  pl.BlockSpec((rows, dk // 2), lambda b, c: (c, 0))],
        out_specs=[pl.BlockSpec((rows, d), lambda b, c: (b * nch + c, 0)),
                   pl.BlockSpec((1, RET_HEADS, dk, dv), lambda b, c: (b, 0, 0, 0))],
        out_shape=[jax.ShapeDtypeStruct((n_p, d), F32),
                   jax.ShapeDtypeStruct((batch, RET_HEADS, dk, dv), F32)],
        compiler_params=_params("arbitrary", "arbitrary"), name="retention_prompt",
    )(xp, g, w_in, w_out, g_gn, cos, sin)


def _pad_rows(x, rows):
    return jnp.concatenate([x, jnp.zeros((rows - x.shape[0], x.shape[1]), x.dtype)], axis=0)


def _ret_sample_kernel(q_ref, k_ref, v_ref, g_ref, ggn_ref, cos_ref, sin_ref, s_ref,
                       o_ref, so_ref, *, t_dec, dk, dv):
    nrow = 2 * t_dec
    cos, sin = cos_ref[...], sin_ref[...]
    r_i = lax.broadcasted_iota(jnp.int32, (nrow, LANES), 0)
    c_i = lax.broadcasted_iota(jnp.int32, (nrow, LANES), 1)
    same_seq = (r_i // t_dec) == (c_i // t_dec)
    diff = ((r_i % t_dec) - (c_i % t_dec)).astype(F32)
    valid = same_seq & (diff >= 0) & (c_i < nrow)
    t_row = (lax.broadcasted_iota(jnp.int32, (nrow, 1), 0) % t_dec).astype(F32)
    seq_row = lax.broadcasted_iota(jnp.int32, (nrow, 1), 0) // t_dec
    seq_lane = lax.broadcasted_iota(jnp.int32, (1, LANES), 1) // t_dec
    outs = []
    for hh in range(RET_HEADS):
        lg = _log_decay(hh)
        q = _rope(q_ref[0, :, hh * dk:(hh + 1) * dk], cos, sin)
        k = _rope(k_ref[0, :, hh * dk:(hh + 1) * dk], cos, sin) * (dk ** -0.5)
        v = _pad_rows(v_ref[0, :, hh * dv:(hh + 1) * dv], LANES).astype(BF16)
        g = g_ref[0, :, hh * dv:(hh + 1) * dv]
        decay = jnp.where(valid, jnp.exp(jnp.maximum(diff, 0.0) * lg), 0.0)
        q16 = _pad_rows(q, 16)
        s = _dot_nt(q16.astype(BF16), _pad_rows(k, LANES).astype(BF16))[:nrow] * decay
        o = _dot(_pad_rows(s, 16).astype(BF16), v)[:nrow]
        q_dec = _pad_rows(q * jnp.exp((t_row + 1.0) * lg), 16).astype(BF16)
        k_dec_t = _pad_rows(k * jnp.exp((t_dec - 1.0 - t_row) * lg), LANES).T
        for sq in range(2):
            state = s_ref[sq, hh]
            o_state = _dot(q_dec, state.astype(BF16))[:nrow]
            o = o + jnp.where(seq_row == sq, o_state, 0.0)
            k_sq = jnp.where(seq_lane == sq, k_dec_t, 0.0).astype(BF16)
            so_ref[sq, hh] = math.exp(t_dec * lg) * state + _dot(k_sq, v)
        outs.append(_group_norm_gate(o, g, ggn_ref[:, hh * dv:(hh + 1) * dv]))
    o_ref[0] = jnp.concatenate(outs, axis=-1)


def _retention_sample(q, k, v, g, g_gn, state, t_dec, past_len, dk, dv):
    n_s = q.shape[0]
    n_seq = n_s // t_dec
    assert n_seq % 2 == 0
    nrow = 2 * t_dec
    pos = past_len + (jnp.arange(nrow, dtype=jnp.int32) % t_dec)
    cos, sin = _rope_tables(pos, dk // 2)
    grp = lambda a: a.reshape(n_seq // 2, nrow, a.shape[1])
    qk, vd = RET_HEADS * dk, RET_HEADS * dv
    kern = functools.partial(_ret_sample_kernel, t_dec=t_dec, dk=dk, dv=dv)
    row_spec = lambda w: pl.BlockSpec((1, nrow, w), lambda i: (i, 0, 0))
    st_spec = pl.BlockSpec((2, RET_HEADS, dk, dv), lambda i: (i, 0, 0, 0))
    gated, s_new = pl.pallas_call(
        kern, grid=(n_seq // 2,),
        in_specs=[row_spec(qk), row_spec(qk), row_spec(vd), row_spec(vd), _resident(g_gn),
                  _resident(cos), _resident(sin), st_spec],
        out_specs=[row_spec(vd), st_spec],
        out_shape=[jax.ShapeDtypeStruct((n_seq // 2, nrow, vd), F32),
                   jax.ShapeDtypeStruct(state.shape, F32)],
        compiler_params=_params("arbitrary"), name="retention_sample",
    )(grp(q), grp(k), grp(v), grp(g), g_gn, cos, sin, state)
    return gated.reshape(n_s, vd), s_new


def _suffix_ones(n):
    r = lax.broadcasted_iota(jnp.int32, (n, n), 0)
    c = lax.broadcasted_iota(jnp.int32, (n, n), 1)
    return jnp.where(r >= c, 1.0, 0.0).astype(BF16)


def _sb_weights(z, carry, ones):
    sp = _softplus(z)
    hi, lo = _split_bf16(sp)
    suffix = _dot(hi, ones) + _dot(lo, ones)
    a = jnp.exp(z - (suffix + carry))
    return a, carry + suffix[:, :1]


def _sb_prompt_kernel(bias_ref, q_ref, k_ref, v_ref, o_ref, *, blk, hd):
    pair, qi = pl.program_id(1), pl.program_id(2)
    lo_lanes = lax.broadcasted_iota(jnp.int32, (1, 2 * hd), 1) < hd
    q = q_ref[...] * (hd ** -0.5)
    zero = jnp.zeros_like(q)
    q_heads = (jnp.where(lo_lanes, q, zero), jnp.where(lo_lanes, zero, q))
    biases = (bias_ref[2 * pair], bias_ref[2 * pair + 1])
    ones = _suffix_ones(blk)
    causal = (lax.broadcasted_iota(jnp.int32, (blk, blk), 1)
              < lax.broadcasted_iota(jnp.int32, (blk, blk), 0))

    def block(kj, state, masked):
        c0, c1, acc = state
        start = pl.multiple_of(kj * blk, blk)
        kb = k_ref[pl.ds(start, blk), :]
        vb = v_ref[pl.ds(start, blk), :]
        vzero = jnp.zeros_like(vb)
        v_heads = (jnp.where(lo_lanes, vb, vzero), jnp.where(lo_lanes, vzero, vb))
        carries = []
        for hh, c in enumerate((c0, c1)):
            z = _dot_nt(q_heads[hh], kb) + biases[hh]
            if masked:
                z = jnp.where(causal, z, MASKED_LOGIT)
            a, c = _sb_weights(z, c, ones)
            acc = acc + _dot(a.astype(BF16), v_heads[hh])
            carries.append(c)
        return carries[0], carries[1], acc

    init = (jnp.zeros((blk, 1), F32), jnp.zeros((blk, 1), F32), jnp.zeros((blk, 2 * hd), F32))
    state = block(qi, init, True)
    state = lax.fori_loop(0, qi, lambda it, st: block(qi - 1 - it, st, False), state)
    o_ref[...] = state[2].astype(o_ref.dtype)


def _sb_prompt(q, k, v, bias, batch, hd):
    n_p, d = q.shape
    t = n_p // batch
    blk = min(SB_BLOCK, t)
    assert t % blk == 0 and 2 * hd == LANES
    nq = t // blk
    kern = functools.partial(_sb_prompt_kernel, blk=blk, hd=hd)
    return pl.pallas_call(
        kern, grid=(batch, d // LANES, nq),
        in_specs=[pl.BlockSpec(memory_space=pltpu.SMEM),
                  pl.BlockSpec((blk, LANES), lambda b, p, i: (b * nq + i, p)),
                  pl.BlockSpec((t, LANES), lambda b, p, i: (b, p)),
                  pl.BlockSpec((t, LANES), lambda b, p, i: (b, p))],
        out_specs=pl.BlockSpec((blk, LANES), lambda b, p, i: (b * nq + i, p)),
        out_shape=jax.ShapeDtypeStruct((n_p, d), BF16),
        compiler_params=_params("arbitrary", "arbitrary", "arbitrary"), name="sb_prompt",
    )(bias, q, k, v)


def _sb_sample_kernel(pt_ref, q_ref, kn_ref, vn_ref, bias_ref, kc_ref, vc_ref, o_ref,
                      qbd_ref, acc_ref, carry_ref, *, t_dec, hd, page):
    del pt_ref
    j = pl.program_id(1)
    nrow = t_dec * SB_HEADS
    d = SB_HEADS * hd
    head_mask = ((lax.broadcasted_iota(jnp.int32, (SB_HEADS, d), 1) // hd)
                 == lax.broadcasted_iota(jnp.int32, (SB_HEADS, d), 0))
    ones = _suffix_ones(page)

    def visit(keys, vals, mask):
        z = _dot_nt(qbd_ref[...], keys) + bias_ref[...]
        if mask is not None:
            z = jnp.where(mask, z, MASKED_LOGIT)
        a, c = _sb_weights(z, carry_ref[:, :1], ones)
        acc_ref[...] += _dot(a.astype(BF16), vals)
        carry_ref[...] = jnp.broadcast_to(c, carry_ref.shape)

    @pl.when(j == 0)
    def _():
        qv = q_ref[0] * (hd ** -0.5)
        qbd_ref[...] = jnp.concatenate(
            [jnp.where(head_mask, qv[t:t + 1, :], 0.0) for t in range(t_dec)], axis=0).astype(BF16)
        acc_ref[...] = jnp.zeros_like(acc_ref)
        carry_ref[...] = jnp.zeros_like(carry_ref)
        key_i = lax.broadcasted_iota(jnp.int32, (nrow, page), 1)
        qry_t = lax.broadcasted_iota(jnp.int32, (nrow, page), 0) // SB_HEADS
        visit(_pad_rows(kn_ref[0], page).astype(BF16), _pad_rows(vn_ref[0], page).astype(BF16),
              key_i < qry_t)

    @pl.when(j > 0)
    def _():
        visit(kc_ref[0].astype(BF16), vc_ref[0].astype(BF16), None)

    @pl.when(j == pl.num_programs(1) - 1)
    def _():
        rows = []
        for t in range(t_dec):
            blk = acc_ref[t * SB_HEADS:(t + 1) * SB_HEADS, :]
            rows.append(jnp.sum(jnp.where(head_mask, blk, 0.0), axis=0, keepdims=True))
        o_ref[0] = jnp.concatenate(rows, axis=0)


def _sb_sample(q, k_new, v_new, cache_k, cache_v, page_table, bias, t_dec, hd):
    n_s, d = q.shape
    n_seq = n_s // t_dec
    n_pool, page = cache_k.shape[0], cache_k.shape[1]
    n_pages = page_table.shape[1]
    assert page == LANES and t_dec <= page
    nrow = t_dec * SB_HEADS
    bias_rows = jnp.broadcast_to(jnp.tile(bias.astype(F32), t_dec)[:, None], (nrow, page))
    seq3 = lambda a: a.reshape(n_seq, t_dec, d)
    seq_spec = pl.BlockSpec((1, t_dec, d), lambda b, j, pt: (b, 0, 0))

    def page_map(b, j, pt):
        return (pt[b * n_pages + n_pages - 1 - jnp.maximum(j - 1, 0)], 0, 0)

    kern = functools.partial(_sb_sample_kernel, t_dec=t_dec, hd=hd, page=page)
    out = pl.pallas_call(
        kern,
        grid_spec=pltpu.PrefetchScalarGridSpec(
            num_scalar_prefetch=1, grid=(n_seq, n_pages + 1),
            in_specs=[seq_spec, seq_spec, seq_spec,
                      pl.BlockSpec((nrow, page), lambda b, j, pt: (0, 0)),
                      pl.BlockSpec((1, page, d), page_map),
                      pl.BlockSpec((1, page, d), page_map)],
            out_specs=seq_spec,
            scratch_shapes=[pltpu.VMEM((nrow, d), BF16), pltpu.VMEM((nrow, d), F32),
                            pltpu.VMEM((nrow, LANES), F32)]),
        out_shape=jax.ShapeDtypeStruct((n_seq, t_dec, d), F32),
        compiler_params=_params("arbitrary", "arbitrary"), name="sb_sample",
    )(page_table.reshape(-1), seq3(q), seq3(k_new), seq3(v_new), bias_rows,
      cache_k.reshape(n_pool, page, d), cache_v.reshape(n_pool, page, d))
    return out.reshape(n_s, d)


def kernel(x_prompt, x_sample, p_prompt, p_sample, state_ret, cache_k, cache_v, page_table, g_norm, w_ff1_gate, w_ff1_up, w_ff1_down, w_ff2_gate, w_ff2_up, w_ff2_down, w_ple_up, w_ple_gate, g_ple, w_ret_in, w_ret_out, g_ret_gn, g_kv, w_kv, w_sb_q, w_sb_out, b_sb, g_final):
    batch, t_prompt, d = x_prompt.shape
    n_dec, t_dec, _ = x_sample.shape
    depth = g_norm.shape[0]
    n_ret = w_ret_in.shape[0]
    dk = d // RET_HEADS
    dv = w_ret_out.shape[1] // RET_HEADS
    hd = d // SB_HEADS
    past_len = page_table.shape[1] * cache_k.shape[1]
    row = lambda a: a.reshape(1, -1)
    bf = lambda a: a.astype(BF16)

    xp = x_prompt.reshape(batch * t_prompt, d)
    xs = x_sample.reshape(n_dec * t_dec, d)
    x_out = [(d, F32)]
    states_p, states_s = [], []
    kv = None
    for i in range(depth):
        (xp, xs), = _tokenwise(_ffn_body, [(xp, xs)],
                               [row(g_norm[i, 0]), bf(w_ff1_gate[i]), bf(w_ff1_up[i]), bf(w_ff1_down[i])],
                               x_out, "ffn1")
        if i < n_ret:
            g_gn = row(g_ret_gn[i])
            xp, s_p = _retention_prompt(xp, batch, row(g_norm[i, 1]), bf(w_ret_in[i]), bf(w_ret_out[i]),
                                        g_gn, dk, dv)
            q, k, v, g = _rowwise(functools.partial(_ret_in_proj_body, qk=RET_HEADS * dk, vd=RET_HEADS * dv),
                                  [xs], [row(g_norm[i, 1]), bf(w_ret_in[i])],
                                  [(RET_HEADS * dk, F32), (RET_HEADS * dk, F32),
                                   (RET_HEADS * dv, F32), (RET_HEADS * dv, F32)], "ret_in_proj_sample")
            gated, s_s = _retention_sample(q, k, v, g, g_gn, state_ret[i], t_dec, past_len, dk, dv)
            xs, = _rowwise(_proj_residual_body, [xs, gated], [bf(w_ret_out[i])], x_out, "ret_out_proj_sample")
            states_p.append(s_p)
            states_s.append(s_s)
        else:
            l = i - n_ret
            (qp, qs), = _tokenwise(_norm_proj_body, [(xp, xs)], [row(g_norm[i, 1]), bf(w_sb_q[l])],
                                   [(d, BF16)], "sb_q_proj")
            (kp32, ks32), (vp32, vs32), (kp16, _), (vp16, _) = kv
            o_p = _sb_prompt(qp, kp16, vp16, b_sb[l].astype(F32), batch, hd)
            o_s = _sb_sample(qs.astype(F32), ks32, vs32, cache_k, cache_v, page_table, b_sb[l], t_dec, hd)
            (xp, xs), = _tokenwise(_proj_residual_body, [(xp, xs), (o_p, bf(o_s))], [bf(w_sb_out[l])],
                                   x_out, "sb_out_proj")
        (xp, xs), = _tokenwise(_ffn_body, [(xp, xs)],
                               [row(g_norm[i, 2]), bf(w_ff2_gate[i]), bf(w_ff2_up[i]), bf(w_ff2_down[i])],
                               x_out, "ffn2")
        p_p = p_prompt[i].reshape(batch * t_prompt, -1)
        p_s = p_sample[i].reshape(n_dec * t_dec, -1)
        (xp, xs), = _tokenwise(_ple_body, [(xp, xs), (p_p, p_s)],
                               [row(g_ple[i]), bf(w_ple_up[i]), bf(w_ple_gate[i])], x_out, "ple")
        if i == n_ret - 1:
            kv = _tokenwise(_kv_body, [(xp, xs)], [row(g_kv), bf(w_kv)],
                            [(d, F32), (d, F32), (d, BF16), (d, BF16)], "kv_proj")
    (yp, ys), = _tokenwise(_final_norm_body, [(xp, xs)], [row(g_final)], x_out, "final_norm")
    (kp32, ks32), (vp32, vs32) = kv[0], kv[1]
    return (yp.reshape(batch, t_prompt, d), ys.reshape(n_dec, t_dec, d),
            jnp.stack(states_p), jnp.stack(states_s),
            kp32.reshape(batch, t_prompt, SB_HEADS, hd), vp32.reshape(batch, t_prompt, SB_HEADS, hd),
            ks32.reshape(n_dec, t_dec, SB_HEADS, hd), vs32.reshape(n_dec, t_dec, SB_HEADS, hd))
```

```python
import functools
import math

import jax
import jax.numpy as jnp
from jax import lax
from jax.experimental import pallas as pl
from jax.experimental.pallas import tpu as pltpu

F32 = jnp.float32
BF16 = jnp.bfloat16

EPS = 1e-6
ROPE_BASE = 10000.0
RET_HEADS = 4
SB_HEADS = 16
LANES = 128
VMEM_LIMIT_BYTES = 58 * 1024 * 1024
MASKED_LOGIT = -1e30
LOG2E = 1.4426950408889634
TOKEN_TILE = 512
RET_CHUNK_ROWS = 256
SB_QUERY_BLOCK = 512
SB_KEY_BLOCK = 256
SB_HEADS_PER_STEP = 4
SB_BLOCKS_PER_ITER = 2


def _params(*sem):
    return pltpu.CompilerParams(dimension_semantics=sem, vmem_limit_bytes=VMEM_LIMIT_BYTES)


def _dot(a, b):
    return jnp.dot(a, b, preferred_element_type=F32)


def _dot_nt(a, b):
    return lax.dot_general(a, b, (((1,), (1,)), ((), ())), preferred_element_type=F32)


def _rmsnorm(x, g):
    y = x * lax.rsqrt(jnp.mean(x * x, axis=-1, keepdims=True) + EPS)
    return y * g


def _silu(x):
    return x * jax.nn.sigmoid(x)


def _softplus(z):
    return jnp.maximum(z, 0.0) + jnp.log(1.0 + jnp.exp(-jnp.abs(z)))


def _split_bf16(x):
    hi = x.astype(BF16)
    lo = (x - hi.astype(F32)).astype(BF16)
    return hi, lo


def _resident(arr):
    nd = arr.ndim
    return pl.BlockSpec(arr.shape, lambda *_: (0,) * nd, pipeline_mode=pl.Buffered(1))


def _tokenwise(body, streams, consts, outs, name):
    n_p, n_s = streams[0][0].shape[0], streams[0][1].shape[0]
    tm = min(TOKEN_TILE, n_s)
    assert n_p % tm == 0 and n_s % tm == 0
    p_tiles, s_tiles = n_p // tm, n_s // tm
    ns, nc, no = len(streams), len(consts), len(outs)

    def kern(*refs):
        in_refs, c_refs, o_refs = refs[:2 * ns], refs[2 * ns:2 * ns + nc], refs[2 * ns + nc:]
        is_prompt = pl.program_id(0) < p_tiles
        vals = [jnp.where(is_prompt, in_refs[2 * k][...], in_refs[2 * k + 1][...]) for k in range(ns)]
        res = body(*vals, *c_refs)

        @pl.when(is_prompt)
        def _():
            for k in range(no):
                o_refs[2 * k][...] = res[k].astype(outs[k][1])

        @pl.when(jnp.logical_not(is_prompt))
        def _():
            for k in range(no):
                o_refs[2 * k + 1][...] = res[k].astype(outs[k][1])

    def p_spec(w):
        return pl.BlockSpec((tm, w), lambda i: (jnp.minimum(i, p_tiles - 1), 0))

    def s_spec(w):
        return pl.BlockSpec((tm, w), lambda i: (jnp.maximum(i - p_tiles, 0), 0))

    in_specs, args = [], []
    for a_p, a_s in streams:
        in_specs += [p_spec(a_p.shape[1]), s_spec(a_s.shape[1])]
        args += [a_p, a_s]
    in_specs += [_resident(c) for c in consts]
    out_specs, out_shape = [], []
    for w, dt in outs:
        out_specs += [p_spec(w), s_spec(w)]
        out_shape += [jax.ShapeDtypeStruct((n_p, w), dt), jax.ShapeDtypeStruct((n_s, w), dt)]
    res = pl.pallas_call(
        kern, grid=(p_tiles + s_tiles,), in_specs=in_specs, out_specs=out_specs,
        out_shape=out_shape, compiler_params=_params("arbitrary"), name=name,
    )(*args, *consts)
    return [(res[2 * k], res[2 * k + 1]) for k in range(no)]


def _rowwise(body, rows, consts, outs, name):
    n = rows[0].shape[0]
    tm = min(TOKEN_TILE, n)
    assert n % tm == 0
    nr, nc, no = len(rows), len(consts), len(outs)

    def kern(*refs):
        res = body(*[r[...] for r in refs[:nr]], *refs[nr:nr + nc])
        for k in range(no):
            refs[nr + nc + k][...] = res[k].astype(outs[k][1])

    in_specs = [pl.BlockSpec((tm, a.shape[1]), lambda i: (i, 0)) for a in rows]
    in_specs += [_resident(c) for c in consts]
    return pl.pallas_call(
        kern, grid=(n // tm,), in_specs=in_specs,
        out_specs=[pl.BlockSpec((tm, w), lambda i: (i, 0)) for w, _ in outs],
        out_shape=[jax.ShapeDtypeStruct((n, w), dt) for w, dt in outs],
        compiler_params=_params("arbitrary"), name=name,
    )(*rows, *consts)


def _ffn_body(x, g_ref, wg_ref, wu_ref, wd_ref):
    h = _rmsnorm(x, g_ref[...]).astype(BF16)
    d_ff = wg_ref.shape[1]
    chunk = 256 if d_ff % 256 == 0 else LANES
    acc = None
    for c0 in range(0, d_ff, chunk):
        gate = _dot(h, wg_ref[:, c0:c0 + chunk])
        up = _dot(h, wu_ref[:, c0:c0 + chunk])
        d = _dot((_silu(gate) * up).astype(BF16), wd_ref[c0:c0 + chunk, :])
        acc = d if acc is None else acc + d
    return (x + 0.5 * acc,)


def _ple_body(x, p, g_ref, wup_ref, wgate_ref):
    h = _rmsnorm(x, g_ref[...]).astype(BF16)
    gate = jax.nn.sigmoid(_dot(h, wgate_ref[...]))
    return (x + _dot(p.astype(BF16), wup_ref[...]) * gate,)


def _sb_q_body(x, g_ref, w_ref, *, scale):
    return (_dot(_rmsnorm(x, g_ref[...]).astype(BF16), w_ref[...]) * scale,)


def _kv_body(x, g_ref, w_ref):
    h = _rmsnorm(x, g_ref[...]).astype(BF16)
    half = w_ref.shape[1] // 2
    k = _dot(h, w_ref[:, :half])
    v = _dot(h, w_ref[:, half:])
    return (k, v, k, v)


def _proj_residual_body(x, a, w_ref):
    return (x + _dot(a.astype(BF16), w_ref[...]),)


def _final_norm_body(x, g_ref):
    return (_rmsnorm(x, g_ref[...]),)


def _ret_in_proj_body(x, g_ref, w_ref, *, qk, vd):
    h = _rmsnorm(x, g_ref[...]).astype(BF16)
    return (_dot(h, w_ref[:, :qk]), _dot(h, w_ref[:, qk:2 * qk]),
            _dot(h, w_ref[:, 2 * qk:2 * qk + vd]), _dot(h, w_ref[:, 2 * qk + vd:]))


def _rope_tables(pos, half):
    inv_freq = ROPE_BASE ** (-jnp.arange(half, dtype=F32) / half)
    ang = pos.astype(F32)[:, None] * inv_freq[None, :]
    return jnp.cos(ang), jnp.sin(ang)


def _rope(x, cos, sin):
    half = x.shape[-1] // 2
    x1, x2 = x[:, :half], x[:, half:]
    return jnp.concatenate([x1 * cos - x2 * sin, x2 * cos + x1 * sin], axis=-1)


def _log_decay(head):
    return math.log(1.0 - 2.0 ** (-5.0 - head))


def _group_norm_gate(o, g, ggn):
    mu = jnp.mean(o, axis=-1, keepdims=True)
    var = jnp.mean(jnp.square(o - mu), axis=-1, keepdims=True)
    on = (o - mu) * lax.rsqrt(var + EPS) * ggn
    return _silu(g) * on


def _ret_prompt_kernel(x_ref, gn_ref, win_ref, wout_ref, ggn_ref, cos_ref, sin_ref,
                       o_ref, s_ref, *, rows, dk, dv):
    qk, vd = RET_HEADS * dk, RET_HEADS * dv

    @pl.when(pl.program_id(1) == 0)
    def _():
        s_ref[...] = jnp.zeros_like(s_ref)

    x = x_ref[...]
    h = _rmsnorm(x, gn_ref[...]).astype(BF16)
    cos, sin = cos_ref[...], sin_ref[...]
    diff = (lax.broadcasted_iota(jnp.int32, (rows, rows), 0)
            - lax.broadcasted_iota(jnp.int32, (rows, rows), 1)).astype(F32)
    ri = lax.broadcasted_iota(jnp.int32, (rows, 1), 0).astype(F32)
    acc = None
    for hh in range(RET_HEADS):
        lg = _log_decay(hh)
        q = _rope(_dot(h, win_ref[:, hh * dk:(hh + 1) * dk]), cos, sin)
        k = _rope(_dot(h, win_ref[:, qk + hh * dk:qk + (hh + 1) * dk]), cos, sin) * (dk ** -0.5)
        v = _dot(h, win_ref[:, 2 * qk + hh * dv:2 * qk + (hh + 1) * dv]).astype(BF16)
        g = _dot(h, win_ref[:, 2 * qk + vd + hh * dv:2 * qk + vd + (hh + 1) * dv])
        decay = jnp.where(diff >= 0, jnp.exp(jnp.maximum(diff, 0.0) * lg), 0.0)
        s = _dot_nt(q.astype(BF16), k.astype(BF16)) * decay
        state = s_ref[0, hh]
        q_dec = (q * jnp.exp((ri + 1.0) * lg)).astype(BF16)
        o = _dot(s.astype(BF16), v) + _dot(q_dec, state.astype(BF16))
        k_dec = k * jnp.exp((rows - 1.0 - ri) * lg)
        s_ref[0, hh] = math.exp(rows * lg) * state + _dot(k_dec.T.astype(BF16), v)
        gated = _group_norm_gate(o, g, ggn_ref[:, hh * dv:(hh + 1) * dv]).astype(BF16)
        d = _dot(gated, wout_ref[hh * dv:(hh + 1) * dv, :])
        acc = d if acc is None else acc + d
    o_ref[...] = x + acc


def _retention_prompt(xp, batch, g, w_in, w_out, g_gn, dk, dv):
    n_p, d = xp.shape
    t = n_p // batch
    rows = min(RET_CHUNK_ROWS, t)
    assert t % rows == 0
    nch = t // rows
    cos, sin = _rope_tables(jnp.arange(t, dtype=jnp.int32), dk // 2)
    kern = functools.partial(_ret_prompt_kernel, rows=rows, dk=dk, dv=dv)
    return pl.pallas_call(
        kern, grid=(batch, nch),
        in_specs=[pl.BlockSpec((rows, d), lambda b, c: (b * nch + c, 0)),
                  _resident(g), _resident(w_in), _resident(w_out), _resident(g_gn),
                  pl.BlockSpec((rows, dk // 2), lambda b, c: (c, 0)),
                  pl.BlockSpec((rows, dk // 2), lambda b, c: (c, 0))],
        out_specs=[pl.BlockSpec((rows, d), lambda b, c: (b * nch + c, 0)),
                   pl.BlockSpec((1, RET_HEADS, dk, dv), lambda b, c: (b, 0, 0, 0))],
        out_shape=[jax.ShapeDtypeStruct((n_p, d), F32),
                   jax.ShapeDtypeStruct((batch, RET_HEADS, dk, dv), F32)],
        compiler_params=_params("arbitrary", "arbitrary"), name="retention_prompt",
    )(xp, g, w_in, w_out, g_gn, cos, sin)


def _pad_rows(x, rows):
    return jnp.concatenate([x, jnp.zeros((rows - x.shape[0], x.shape[1]), x.dtype)], axis=0)


def _ret_sample_kernel(q_ref, k_ref, v_ref, g_ref, ggn_ref, cos_ref, sin_ref, s_ref, *rest,
                       t_dec, dk, dv, n_fill):
    o_ref, so_ref = rest[-2], rest[-1]
    nrow = 2 * t_dec
    cos, sin = cos_ref[...], sin_ref[...]
    r_i = lax.broadcasted_iota(jnp.int32, (nrow, LANES), 0)
    c_i = lax.broadcasted_iota(jnp.int32, (nrow, LANES), 1)
    same_seq = (r_i // t_dec) == (c_i // t_dec)
    diff = ((r_i % t_dec) - (c_i % t_dec)).astype(F32)
    valid = same_seq & (diff >= 0) & (c_i < nrow)
    t_row = (lax.broadcasted_iota(jnp.int32, (nrow, 1), 0) % t_dec).astype(F32)
    seq_row = lax.broadcasted_iota(jnp.int32, (nrow, 1), 0) // t_dec
    seq_lane = lax.broadcasted_iota(jnp.int32, (1, LANES), 1) // t_dec
    for fill in range(n_fill):
        so_ref[1 + fill] = jnp.zeros(so_ref.shape[1:], so_ref.dtype)
    outs = []
    for hh in range(RET_HEADS):
        lg = _log_decay(hh)
        q = _rope(q_ref[0, :, hh * dk:(hh + 1) * dk], cos, sin)
        k = _rope(k_ref[0, :, hh * dk:(hh + 1) * dk], cos, sin) * (dk ** -0.5)
        v = _pad_rows(v_ref[0, :, hh * dv:(hh + 1) * dv], LANES).astype(BF16)
        g = g_ref[0, :, hh * dv:(hh + 1) * dv]
        decay = jnp.where(valid, jnp.exp(jnp.maximum(diff, 0.0) * lg), 0.0)
        q16 = _pad_rows(q, 16)
        s = _dot_nt(q16.astype(BF16), _pad_rows(k, LANES).astype(BF16))[:nrow] * decay
        o = _dot(_pad_rows(s, 16).astype(BF16), v)[:nrow]
        q_dec = _pad_rows(q * jnp.exp((t_row + 1.0) * lg), 16).astype(BF16)
        k_dec_t = _pad_rows(k * jnp.exp((t_dec - 1.0 - t_row) * lg), LANES).T
        for sq in range(2):
            state = s_ref[0, sq, hh]
            o_state = _dot(q_dec, state.astype(BF16))[:nrow]
            o = o + jnp.where(seq_row == sq, o_state, 0.0)
            k_sq = jnp.where(seq_lane == sq, k_dec_t, 0.0).astype(BF16)
            so_ref[0, sq, hh] = math.exp(t_dec * lg) * state + _dot(k_sq, v)
        outs.append(_group_norm_gate(o, g, ggn_ref[:, hh * dv:(hh + 1) * dv]))
    o_ref[0] = jnp.concatenate(outs, axis=-1)


def _retention_sample(q, k, v, g, g_gn, state_all, layer, prev_states, t_dec, past_len, dk, dv):
    n_s = q.shape[0]
    n_seq = n_s // t_dec
    n_layers = state_all.shape[0]
    assert n_seq % 2 == 0
    nrow = 2 * t_dec
    pos = past_len + (jnp.arange(nrow, dtype=jnp.int32) % t_dec)
    cos, sin = _rope_tables(pos, dk // 2)
    grp = lambda a: a.reshape(n_seq // 2, nrow, a.shape[1])
    qk, vd = RET_HEADS * dk, RET_HEADS * dv
    first = prev_states is None
    n_fill = n_layers - 1 if first else 0
    kern = functools.partial(_ret_sample_kernel, t_dec=t_dec, dk=dk, dv=dv, n_fill=n_fill)
    row_spec = lambda w: pl.BlockSpec((1, nrow, w), lambda i: (i, 0, 0))
    st_in = pl.BlockSpec((1, 2, RET_HEADS, dk, dv), lambda i: (layer, i, 0, 0, 0))
    st_out = pl.BlockSpec((1 + n_fill, 2, RET_HEADS, dk, dv), lambda i: (0 if first else layer, i, 0, 0, 0))
    in_specs = [row_spec(qk), row_spec(qk), row_spec(vd), row_spec(vd), _resident(g_gn),
                _resident(cos), _resident(sin), st_in]
    args = [grp(q), grp(k), grp(v), grp(g), g_gn, cos, sin, state_all]
    aliases = {}
    if not first:
        in_specs.append(pl.BlockSpec(memory_space=pl.ANY))
        args.append(prev_states)
        aliases = {len(args) - 1: 1}
    gated, s_new = pl.pallas_call(
        kern, grid=(n_seq // 2,), in_specs=in_specs,
        out_specs=[row_spec(vd), st_out],
        out_shape=[jax.ShapeDtypeStruct((n_seq // 2, nrow, vd), F32),
                   jax.ShapeDtypeStruct(state_all.shape, F32)],
        input_output_aliases=aliases,
        compiler_params=_params("arbitrary"), name="retention_sample",
    )(*args)
    return gated.reshape(n_s, vd), s_new


def _suffix_ones(n):
    r = lax.broadcasted_iota(jnp.int32, (n, n), 0)
    c = lax.broadcasted_iota(jnp.int32, (n, n), 1)
    return jnp.where(r >= c, 1.0, 0.0).astype(BF16)


def _neg_abs(y):
    bits = lax.bitcast_convert_type(y, jnp.int32) | jnp.int32(-2 ** 31)
    return lax.bitcast_convert_type(bits, F32)


def _sb_blocks(ys, ones):
    sps = [jnp.maximum(y, 0.0) + jnp.log(1.0 + jnp.exp2(_neg_abs(y))) * LOG2E for y in ys]
    parts = [_split_bf16(sp) for sp in sps]
    suffixes = [_dot(hi, ones) + _dot(lo, ones) for hi, lo in parts]
    return [(jnp.exp2(y - sfx), sfx[:, :1]) for y, sfx in zip(ys, suffixes)]


def _sb_prompt_kernel(bias_ref, q_ref, k_ref, v_ref, o_ref, qh_ref, ones_ref, bm_ref, acc_ref, carry_ref,
                      *, tq, tk, hd, n_pairs):
    grp, qi = pl.program_id(1), pl.program_id(2)
    ratio = tq // tk
    lo_lanes = lax.broadcasted_iota(jnp.int32, (1, LANES), 1) < hd
    row = lax.broadcasted_iota(jnp.int32, (tq, tk), 0)
    col = lax.broadcasted_iota(jnp.int32, (tq, tk), 1)
    for p in range(n_pairs):
        qp = q_ref[:, p * LANES:(p + 1) * LANES]
        zero = jnp.zeros_like(qp)
        qh_ref[2 * p] = jnp.where(lo_lanes, qp, zero)
        qh_ref[2 * p + 1] = jnp.where(lo_lanes, zero, qp)
        for hh in range(2):
            bias = bias_ref[(grp * n_pairs + p) * 2 + hh]
            bm_ref[2 * p + hh, 0] = jnp.full((tq, tk), bias, F32)
            for d in range(ratio):
                bm_ref[2 * p + hh, 1 + d] = jnp.where(col + d * tk < row, bias, MASKED_LOGIT)
    ones_ref[...] = _suffix_ones(tk)
    acc_ref[...] = jnp.zeros_like(acc_ref)
    carry_ref[...] = jnp.zeros_like(carry_ref)
    n_blocks = ratio * qi + ratio

    n_heads = 2 * n_pairs

    def body(it, _):
        ones = ones_ref[...]
        starts, ys = [], []
        for u in range(SB_BLOCKS_PER_ITER):
            kj = n_blocks - 1 - (it * SB_BLOCKS_PER_ITER + u)
            start = pl.multiple_of(kj * tk, tk)
            kind = jnp.clip(kj - ratio * qi + 1, 0, ratio)
            kbs = [k_ref[pl.ds(start, tk), p * LANES:(p + 1) * LANES] for p in range(n_pairs)]
            ys += [_dot_nt(qh_ref[h], kbs[h // 2]) + bm_ref[h, kind] for h in range(n_heads)]
            starts.append(start)
        blocks = _sb_blocks(ys, ones)
        contribs = []
        for u, start in enumerate(starts):
            for p in range(n_pairs):
                vb = v_ref[pl.ds(start, tk), p * LANES:(p + 1) * LANES]
                vzero = jnp.zeros_like(vb)
                a0, a1 = blocks[u * n_heads + 2 * p][0], blocks[u * n_heads + 2 * p + 1][0]
                contribs.append(_dot(a0.astype(BF16), jnp.where(lo_lanes, vb, vzero))
                                + _dot(a1.astype(BF16), jnp.where(lo_lanes, vzero, vb)))
        for p in range(n_pairs):
            c = carry_ref[p]
            acc = acc_ref[:, p * LANES:(p + 1) * LANES]
            for u in range(SB_BLOCKS_PER_ITER):
                acc = acc + jnp.exp2(-c) * contribs[u * n_pairs + p]
                c = c + jnp.where(lo_lanes, blocks[u * n_heads + 2 * p][1], blocks[u * n_heads + 2 * p + 1][1])
            acc_ref[:, p * LANES:(p + 1) * LANES] = acc
            carry_ref[p] = c
        return 0

    lax.fori_loop(0, n_blocks // SB_BLOCKS_PER_ITER, body, 0)
    o_ref[...] = acc_ref[...].astype(o_ref.dtype)


def _sb_prompt(q, k, v, bias, batch, hd):
    n_p, d = q.shape
    t = n_p // batch
    tq, tk = min(SB_QUERY_BLOCK, t), min(SB_KEY_BLOCK, t)
    n_pairs = SB_HEADS_PER_STEP // 2
    width = n_pairs * LANES
    assert t % tq == 0 and tq % (tk * SB_BLOCKS_PER_ITER) == 0 and 2 * hd == LANES and d % width == 0
    nq = t // tq
    kern = functools.partial(_sb_prompt_kernel, tq=tq, tk=tk, hd=hd, n_pairs=n_pairs)
    return pl.pallas_call(
        kern, grid=(batch, d // width, nq),
        in_specs=[pl.BlockSpec(memory_space=pltpu.SMEM),
                  pl.BlockSpec((tq, width), lambda b, g, i: (b * nq + i, g)),
                  pl.BlockSpec((t, width), lambda b, g, i: (b, g)),
                  pl.BlockSpec((t, width), lambda b, g, i: (b, g))],
        out_specs=pl.BlockSpec((tq, width), lambda b, g, i: (b * nq + i, g)),
        out_shape=jax.ShapeDtypeStruct((n_p, d), BF16),
        scratch_shapes=[pltpu.VMEM((2 * n_pairs, tq, LANES), BF16), pltpu.VMEM((tk, tk), BF16),
                        pltpu.VMEM((2 * n_pairs, 1 + tq // tk, tq, tk), F32), pltpu.VMEM((tq, width), F32),
                        pltpu.VMEM((n_pairs, tq, LANES), F32)],
        compiler_params=_params("arbitrary", "arbitrary", "arbitrary"), name="sb_prompt",
    )(bias, q, k, v)


def _sb_sample_kernel(pt_ref, q_ref, kn_ref, vn_ref, bias_ref, *refs, t_dec, hd, n_pages):
    del pt_ref
    k_refs, v_refs, o_ref = refs[:n_pages], refs[n_pages:2 * n_pages], refs[2 * n_pages]
    page = k_refs[0].shape[1]
    nrow = t_dec * SB_HEADS
    d = SB_HEADS * hd
    head_mask = ((lax.broadcasted_iota(jnp.int32, (SB_HEADS, d), 1) // hd)
                 == lax.broadcasted_iota(jnp.int32, (SB_HEADS, d), 0))
    ones = _suffix_ones(page)
    bias = bias_ref[...]
    qv = q_ref[0]
    qbd = jnp.concatenate([jnp.where(head_mask, qv[t:t + 1, :], 0.0) for t in range(t_dec)],
                          axis=0).astype(BF16)
    key_i = lax.broadcasted_iota(jnp.int32, (nrow, page), 1)
    qry_t = lax.broadcasted_iota(jnp.int32, (nrow, page), 0) // SB_HEADS
    y_new = _dot_nt(qbd, _pad_rows(kn_ref[0], page).astype(BF16)) + bias
    ys = [jnp.where(key_i < qry_t, y_new, MASKED_LOGIT)] + [_dot_nt(qbd, kr[0]) + bias for kr in k_refs]
    vals = [_pad_rows(vn_ref[0], page).astype(BF16)] + [vr[0] for vr in v_refs]
    blocks = _sb_blocks(ys, ones)
    contribs = [_dot(a.astype(BF16), v) for (a, _), v in zip(blocks, vals)]
    acc = jnp.zeros((nrow, d), F32)
    carry = jnp.zeros((nrow, 1), F32)
    for contrib, (_, tot) in zip(contribs, blocks):
        acc = acc + jnp.exp2(-carry) * contrib
        carry = carry + tot
    rows = []
    for t in range(t_dec):
        blk = acc[t * SB_HEADS:(t + 1) * SB_HEADS, :]
        rows.append(jnp.sum(jnp.where(head_mask, blk, 0.0), axis=0, keepdims=True))
    o_ref[0] = jnp.concatenate(rows, axis=0)


def _sb_sample(q, k_new, v_new, pages_k, pages_v, page_table, bias, t_dec, hd):
    n_s, d = q.shape
    n_seq = n_s // t_dec
    page = pages_k.shape[1]
    n_pages = page_table.shape[1]
    assert page == LANES and t_dec <= page
    nrow = t_dec * SB_HEADS
    bias_rows = jnp.broadcast_to(jnp.tile(bias.astype(F32), t_dec)[:, None], (nrow, page))
    seq3 = lambda a: a.reshape(n_seq, t_dec, d)
    seq_spec = pl.BlockSpec((1, t_dec, d), lambda b, pt: (b, 0, 0))

    def page_spec(i):
        return pl.BlockSpec((1, page, d), lambda b, pt: (pt[b * n_pages + n_pages - 1 - i], 0, 0))

    kern = functools.partial(_sb_sample_kernel, t_dec=t_dec, hd=hd, n_pages=n_pages)
    out = pl.pallas_call(
        kern,
        grid_spec=pltpu.PrefetchScalarGridSpec(
            num_scalar_prefetch=1, grid=(n_seq,),
            in_specs=[seq_spec, seq_spec, seq_spec, pl.BlockSpec((nrow, page), lambda b, pt: (0, 0))]
                     + [page_spec(i) for i in range(n_pages)] + [page_spec(i) for i in range(n_pages)],
            out_specs=seq_spec),
        out_shape=jax.ShapeDtypeStruct((n_seq, t_dec, d), F32),
        compiler_params=_params("arbitrary"), name="sb_sample",
    )(page_table.reshape(-1), seq3(q), seq3(k_new), seq3(v_new), bias_rows,
      *([pages_k] * n_pages), *([pages_v] * n_pages))
    return out.reshape(n_s, d)


def kernel(x_prompt, x_sample, p_prompt, p_sample, state_ret, cache_k, cache_v, page_table, g_norm, w_ff1_gate, w_ff1_up, w_ff1_down, w_ff2_gate, w_ff2_up, w_ff2_down, w_ple_up, w_ple_gate, g_ple, w_ret_in, w_ret_out, g_ret_gn, g_kv, w_kv, w_sb_q, w_sb_out, b_sb, g_final):
    batch, t_prompt, d = x_prompt.shape
    n_dec, t_dec, _ = x_sample.shape
    depth = g_norm.shape[0]
    n_ret = w_ret_in.shape[0]
    dk = d // RET_HEADS
    dv = w_ret_out.shape[1] // RET_HEADS
    hd = d // SB_HEADS
    past_len = page_table.shape[1] * cache_k.shape[1]
    row = lambda a: a.reshape(1, -1)
    bf = lambda a: a.astype(BF16)

    xp = x_prompt.reshape(batch * t_prompt, d)
    xs = x_sample.reshape(n_dec * t_dec, d)
    x_out = [(d, F32)]
    states_p, states_s = [], None
    kv = None
    n_pool, page = cache_k.shape[0], cache_k.shape[1]
    pages_k = bf(cache_k.reshape(n_pool, page, d))
    pages_v = bf(cache_v.reshape(n_pool, page, d))
    for i in range(depth):
        (xp, xs), = _tokenwise(_ffn_body, [(xp, xs)],
                               [row(g_norm[i, 0]), bf(w_ff1_gate[i]), bf(w_ff1_up[i]), bf(w_ff1_down[i])],
                               x_out, "ffn1")
        if i < n_ret:
            g_gn = row(g_ret_gn[i])
            xp, s_p = _retention_prompt(xp, batch, row(g_norm[i, 1]), bf(w_ret_in[i]), bf(w_ret_out[i]),
                                        g_gn, dk, dv)
            q, k, v, g = _rowwise(functools.partial(_ret_in_proj_body, qk=RET_HEADS * dk, vd=RET_HEADS * dv),
                                  [xs], [row(g_norm[i, 1]), bf(w_ret_in[i])],
                                  [(RET_HEADS * dk, F32), (RET_HEADS * dk, F32),
                                   (RET_HEADS * dv, F32), (RET_HEADS * dv, F32)], "ret_in_proj_sample")
            gated, states_s = _retention_sample(q, k, v, g, g_gn, state_ret, i, states_s, t_dec, past_len, dk, dv)
            xs, = _rowwise(_proj_residual_body, [xs, gated], [bf(w_ret_out[i])], x_out, "ret_out_proj_sample")
            states_p.append(s_p)
        else:
            l = i - n_ret
            (qp, qs), = _tokenwise(functools.partial(_sb_q_body, scale=LOG2E * hd ** -0.5), [(xp, xs)],
                                   [row(g_norm[i, 1]), bf(w_sb_q[l])], [(d, BF16)], "sb_q_proj")
            (kp32, ks32), (vp32, vs32), (kp16, _), (vp16, _) = kv
            bias2 = b_sb[l].astype(F32) * LOG2E
            o_p = _sb_prompt(qp, kp16, vp16, bias2, batch, hd)
            o_s = _sb_sample(qs.astype(F32), ks32, vs32, pages_k, pages_v, page_table, bias2, t_dec, hd)
            (xp, xs), = _tokenwise(_proj_residual_body, [(xp, xs), (o_p, bf(o_s))], [bf(w_sb_out[l])],
                                   x_out, "sb_out_proj")
        (xp, xs), = _tokenwise(_ffn_body, [(xp, xs)],
                               [row(g_norm[i, 2]), bf(w_ff2_gate[i]), bf(w_ff2_up[i]), bf(w_ff2_down[i])],
                               x_out, "ffn2")
        p_p = p_prompt[i].reshape(batch * t_prompt, -1)
        p_s = p_sample[i].reshape(n_dec * t_dec, -1)
        (xp, xs), = _tokenwise(_ple_body, [(xp, xs), (p_p, p_s)],
                               [row(g_ple[i]), bf(w_ple_up[i]), bf(w_ple_gate[i])], x_out, "ple")
        if i == n_ret - 1:
            kv = _tokenwise(_kv_body, [(xp, xs)], [row(g_kv), bf(w_kv)],
                            [(d, F32), (d, F32), (d, BF16), (d, BF16)], "kv_proj")
    (yp, ys), = _tokenwise(_final_norm_body, [(xp, xs)], [row(g_final)], x_out, "final_norm")
    (kp32, ks32), (vp32, vs32) = kv[0], kv[1]
    return (yp.reshape(batch, t_prompt, d), ys.reshape(n_dec, t_dec, d),
            jnp.stack(states_p), states_s,
            kp32.reshape(batch, t_prompt, SB_HEADS, hd), vp32.reshape(batch, t_prompt, SB_HEADS, hd),
            ks32.reshape(n_dec, t_dec, SB_HEADS, hd), vs32.reshape(n_dec, t_dec, SB_HEADS, hd))
```

```python
import functools
import math

import jax
import jax.numpy as jnp
from jax import lax
from jax.experimental import pallas as pl
from jax.experimental.pallas import tpu as pltpu

F32 = jnp.float32
BF16 = jnp.bfloat16

EPS = 1e-6
ROPE_BASE = 10000.0
RET_HEADS = 4
SB_HEADS = 16
LANES = 128
VMEM_LIMIT_BYTES = 58 * 1024 * 1024
MASKED_LOGIT = -1e30
LOG2E = 1.4426950408889634
TOKEN_TILE = 512
RET_CHUNK_ROWS = 256
SB_QUERY_BLOCK = 512
SB_KEY_BLOCK = 256
SB_HEADS_PER_STEP = 4
SB_BLOCKS_PER_ITER = 2


def _params(*sem, flags=None):
    return pltpu.CompilerParams(dimension_semantics=sem, vmem_limit_bytes=VMEM_LIMIT_BYTES, flags=flags)


def _dot(a, b):
    return jnp.dot(a, b, preferred_element_type=F32)


def _dot_nt(a, b):
    return lax.dot_general(a, b, (((1,), (1,)), ((), ())), preferred_element_type=F32)


def _rmsnorm(x, g):
    y = x * lax.rsqrt(jnp.mean(x * x, axis=-1, keepdims=True) + EPS)
    return y * g


def _silu(x):
    return x * jax.nn.sigmoid(x)


def _softplus(z):
    return jnp.maximum(z, 0.0) + jnp.log(1.0 + jnp.exp(-jnp.abs(z)))


def _split_bf16(x):
    hi = x.astype(BF16)
    lo = (x - hi.astype(F32)).astype(BF16)
    return hi, lo


def _resident(arr):
    nd = arr.ndim
    return pl.BlockSpec(arr.shape, lambda *_: (0,) * nd, pipeline_mode=pl.Buffered(1))


def _tokenwise(body, streams, consts, outs, name):
    n_p, n_s = streams[0][0].shape[0], streams[0][1].shape[0]
    tm = min(TOKEN_TILE, n_s)
    assert n_p % tm == 0 and n_s % tm == 0
    p_tiles, s_tiles = n_p // tm, n_s // tm
    ns, nc, no = len(streams), len(consts), len(outs)

    def kern(*refs):
        in_refs, c_refs, o_refs = refs[:2 * ns], refs[2 * ns:2 * ns + nc], refs[2 * ns + nc:]
        is_prompt = pl.program_id(0) < p_tiles
        vals = [jnp.where(is_prompt, in_refs[2 * k][...], in_refs[2 * k + 1][...]) for k in range(ns)]
        res = body(*vals, *c_refs)

        @pl.when(is_prompt)
        def _():
            for k in range(no):
                o_refs[2 * k][...] = res[k].astype(outs[k][1])

        @pl.when(jnp.logical_not(is_prompt))
        def _():
            for k in range(no):
                o_refs[2 * k + 1][...] = res[k].astype(outs[k][1])

    def p_spec(w):
        return pl.BlockSpec((tm, w), lambda i: (jnp.minimum(i, p_tiles - 1), 0))

    def s_spec(w):
        return pl.BlockSpec((tm, w), lambda i: (jnp.maximum(i - p_tiles, 0), 0))

    in_specs, args = [], []
    for a_p, a_s in streams:
        in_specs += [p_spec(a_p.shape[1]), s_spec(a_s.shape[1])]
        args += [a_p, a_s]
    in_specs += [_resident(c) for c in consts]
    out_specs, out_shape = [], []
    for w, dt in outs:
        out_specs += [p_spec(w), s_spec(w)]
        out_shape += [jax.ShapeDtypeStruct((n_p, w), dt), jax.ShapeDtypeStruct((n_s, w), dt)]
    res = pl.pallas_call(
        kern, grid=(p_tiles + s_tiles,), in_specs=in_specs, out_specs=out_specs,
        out_shape=out_shape, compiler_params=_params("arbitrary"), name=name,
    )(*args, *consts)
    return [(res[2 * k], res[2 * k + 1]) for k in range(no)]


def _rowwise(body, rows, consts, outs, name):
    n = rows[0].shape[0]
    tm = min(TOKEN_TILE, n)
    assert n % tm == 0
    nr, nc, no = len(rows), len(consts), len(outs)

    def kern(*refs):
        res = body(*[r[...] for r in refs[:nr]], *refs[nr:nr + nc])
        for k in range(no):
            refs[nr + nc + k][...] = res[k].astype(outs[k][1])

    in_specs = [pl.BlockSpec((tm, a.shape[1]), lambda i: (i, 0)) for a in rows]
    in_specs += [_resident(c) for c in consts]
    return pl.pallas_call(
        kern, grid=(n // tm,), in_specs=in_specs,
        out_specs=[pl.BlockSpec((tm, w), lambda i: (i, 0)) for w, _ in outs],
        out_shape=[jax.ShapeDtypeStruct((n, w), dt) for w, dt in outs],
        compiler_params=_params("arbitrary"), name=name,
    )(*rows, *consts)


def _ffn_body(x, g_ref, wg_ref, wu_ref, wd_ref):
    h = _rmsnorm(x, g_ref[...]).astype(BF16)
    d_ff = wg_ref.shape[1]
    chunk = 256 if d_ff % 256 == 0 else LANES
    acc = None
    for c0 in range(0, d_ff, chunk):
        gate = _dot(h, wg_ref[:, c0:c0 + chunk])
        up = _dot(h, wu_ref[:, c0:c0 + chunk])
        d = _dot((_silu(gate) * up).astype(BF16), wd_ref[c0:c0 + chunk, :])
        acc = d if acc is None else acc + d
    return (x + 0.5 * acc,)


def _ple_body(x, p, g_ref, wup_ref, wgate_ref):
    h = _rmsnorm(x, g_ref[...]).astype(BF16)
    gate = jax.nn.sigmoid(_dot(h, wgate_ref[...]))
    return (x + _dot(p.astype(BF16), wup_ref[...]) * gate,)


def _sb_q_body(x, g_ref, w_ref, *, scale):
    return (_dot(_rmsnorm(x, g_ref[...]).astype(BF16), w_ref[...]) * scale,)


def _kv_body(x, g_ref, w_ref):
    h = _rmsnorm(x, g_ref[...]).astype(BF16)
    half = w_ref.shape[1] // 2
    k = _dot(h, w_ref[:, :half])
    v = _dot(h, w_ref[:, half:])
    return (k, v, k, v)


def _proj_residual_body(x, a, w_ref):
    return (x + _dot(a.astype(BF16), w_ref[...]),)


def _final_norm_body(x, g_ref):
    return (_rmsnorm(x, g_ref[...]),)


def _ret_in_proj_body(x, g_ref, w_ref, *, qk, vd):
    h = _rmsnorm(x, g_ref[...]).astype(BF16)
    return (_dot(h, w_ref[:, :qk]), _dot(h, w_ref[:, qk:2 * qk]),
            _dot(h, w_ref[:, 2 * qk:2 * qk + vd]), _dot(h, w_ref[:, 2 * qk + vd:]))


def _rope_tables(pos, half):
    inv_freq = ROPE_BASE ** (-jnp.arange(half, dtype=F32) / half)
    ang = pos.astype(F32)[:, None] * inv_freq[None, :]
    return jnp.cos(ang), jnp.sin(ang)


def _rope(x, cos, sin):
    half = x.shape[-1] // 2
    x1, x2 = x[:, :half], x[:, half:]
    return jnp.concatenate([x1 * cos - x2 * sin, x2 * cos + x1 * sin], axis=-1)


def _log_decay(head):
    return math.log(1.0 - 2.0 ** (-5.0 - head))


def _group_norm_gate(o, g, ggn):
    mu = jnp.mean(o, axis=-1, keepdims=True)
    var = jnp.mean(jnp.square(o - mu), axis=-1, keepdims=True)
    on = (o - mu) * lax.rsqrt(var + EPS) * ggn
    return _silu(g) * on


def _ret_prompt_kernel(x_ref, gn_ref, win_ref, wout_ref, ggn_ref, cos_ref, sin_ref,
                       o_ref, s_ref, *, rows, dk, dv):
    qk, vd = RET_HEADS * dk, RET_HEADS * dv

    @pl.when(pl.program_id(1) == 0)
    def _():
        s_ref[...] = jnp.zeros_like(s_ref)

    x = x_ref[...]
    h = _rmsnorm(x, gn_ref[...]).astype(BF16)
    cos, sin = cos_ref[...], sin_ref[...]
    diff = (lax.broadcasted_iota(jnp.int32, (rows, rows), 0)
            - lax.broadcasted_iota(jnp.int32, (rows, rows), 1)).astype(F32)
    ri = lax.broadcasted_iota(jnp.int32, (rows, 1), 0).astype(F32)
    acc = None
    for hh in range(RET_HEADS):
        lg = _log_decay(hh)
        q = _rope(_dot(h, win_ref[:, hh * dk:(hh + 1) * dk]), cos, sin)
        k = _rope(_dot(h, win_ref[:, qk + hh * dk:qk + (hh + 1) * dk]), cos, sin) * (dk ** -0.5)
        v = _dot(h, win_ref[:, 2 * qk + hh * dv:2 * qk + (hh + 1) * dv]).astype(BF16)
        g = _dot(h, win_ref[:, 2 * qk + vd + hh * dv:2 * qk + vd + (hh + 1) * dv])
        decay = jnp.where(diff >= 0, jnp.exp(jnp.maximum(diff, 0.0) * lg), 0.0)
        s = _dot_nt(q.astype(BF16), k.astype(BF16)) * decay
        state = s_ref[0, hh]
        q_dec = (q * jnp.exp((ri + 1.0) * lg)).astype(BF16)
        o = _dot(s.astype(BF16), v) + _dot(q_dec, state.astype(BF16))
        k_dec = k * jnp.exp((rows - 1.0 - ri) * lg)
        s_ref[0, hh] = math.exp(rows * lg) * state + _dot(k_dec.T.astype(BF16), v)
        gated = _group_norm_gate(o, g, ggn_ref[:, hh * dv:(hh + 1) * dv]).astype(BF16)
        d = _dot(gated, wout_ref[hh * dv:(hh + 1) * dv, :])
        acc = d if acc is None else acc + d
    o_ref[...] = x + acc


def _retention_prompt(xp, batch, g, w_in, w_out, g_gn, dk, dv):
    n_p, d = xp.shape
    t = n_p // batch
    rows = min(RET_CHUNK_ROWS, t)
    assert t % rows == 0
    nch = t // rows
    cos, sin = _rope_tables(jnp.arange(t, dtype=jnp.int32), dk // 2)
    kern = functools.partial(_ret_prompt_kernel, rows=rows, dk=dk, dv=dv)
    return pl.pallas_call(
        kern, grid=(batch, nch),
        in_specs=[pl.BlockSpec((rows, d), lambda b, c: (b * nch + c, 0)),
                  _resident(g), _resident(w_in), _resident(w_out), _resident(g_gn),
                  pl.BlockSpec((rows, dk // 2), lambda b, c: (c, 0)),
                  pl.BlockSpec((rows, dk // 2), lambda b, c: (c, 0))],
        out_specs=[pl.BlockSpec((rows, d), lambda b, c: (b * nch + c, 0)),
                   pl.BlockSpec((1, RET_HEADS, dk, dv), lambda b, c: (b, 0, 0, 0))],
        out_shape=[jax.ShapeDtypeStruct((n_p, d), F32),
                   jax.ShapeDtypeStruct((batch, RET_HEADS, dk, dv), F32)],
        compiler_params=_params("arbitrary", "arbitrary"), name="retention_prompt",
    )(xp, g, w_in, w_out, g_gn, cos, sin)


def _pad_rows(x, rows):
    return jnp.concatenate([x, jnp.zeros((rows - x.shape[0], x.shape[1]), x.dtype)], axis=0)


def _ret_sample_kernel(q_ref, k_ref, v_ref, g_ref, ggn_ref, cos_ref, sin_ref, s_ref, *rest,
                       t_dec, dk, dv, n_fill):
    o_ref, so_ref = rest[-2], rest[-1]
    nrow = 2 * t_dec
    cos, sin = cos_ref[...], sin_ref[...]
    r_i = lax.broadcasted_iota(jnp.int32, (nrow, LANES), 0)
    c_i = lax.broadcasted_iota(jnp.int32, (nrow, LANES), 1)
    same_seq = (r_i // t_dec) == (c_i // t_dec)
    diff = ((r_i % t_dec) - (c_i % t_dec)).astype(F32)
    valid = same_seq & (diff >= 0) & (c_i < nrow)
    t_row = (lax.broadcasted_iota(jnp.int32, (nrow, 1), 0) % t_dec).astype(F32)
    seq_row = lax.broadcasted_iota(jnp.int32, (nrow, 1), 0) // t_dec
    seq_lane = lax.broadcasted_iota(jnp.int32, (1, LANES), 1) // t_dec
    for fill in range(n_fill):
        so_ref[1 + fill] = jnp.zeros(so_ref.shape[1:], so_ref.dtype)
    outs = []
    for hh in range(RET_HEADS):
        lg = _log_decay(hh)
        q = _rope(q_ref[0, :, hh * dk:(hh + 1) * dk], cos, sin)
        k = _rope(k_ref[0, :, hh * dk:(hh + 1) * dk], cos, sin) * (dk ** -0.5)
        v = _pad_rows(v_ref[0, :, hh * dv:(hh + 1) * dv], LANES).astype(BF16)
        g = g_ref[0, :, hh * dv:(hh + 1) * dv]
        decay = jnp.where(valid, jnp.exp(jnp.maximum(diff, 0.0) * lg), 0.0)
        q16 = _pad_rows(q, 16)
        s = _dot_nt(q16.astype(BF16), _pad_rows(k, LANES).astype(BF16))[:nrow] * decay
        o = _dot(_pad_rows(s, 16).astype(BF16), v)[:nrow]
        q_dec = _pad_rows(q * jnp.exp((t_row + 1.0) * lg), 16).astype(BF16)
        k_dec_t = _pad_rows(k * jnp.exp((t_dec - 1.0 - t_row) * lg), LANES).T
        for sq in range(2):
            state = s_ref[0, sq, hh]
            o_state = _dot(q_dec, state.astype(BF16))[:nrow]
            o = o + jnp.where(seq_row == sq, o_state, 0.0)
            k_sq = jnp.where(seq_lane == sq, k_dec_t, 0.0).astype(BF16)
            so_ref[0, sq, hh] = math.exp(t_dec * lg) * state + _dot(k_sq, v)
        outs.append(_group_norm_gate(o, g, ggn_ref[:, hh * dv:(hh + 1) * dv]))
    o_ref[0] = jnp.concatenate(outs, axis=-1)


def _retention_sample(q, k, v, g, g_gn, state_all, layer, prev_states, t_dec, past_len, dk, dv):
    n_s = q.shape[0]
    n_seq = n_s // t_dec
    n_layers = state_all.shape[0]
    assert n_seq % 2 == 0
    nrow = 2 * t_dec
    pos = past_len + (jnp.arange(nrow, dtype=jnp.int32) % t_dec)
    cos, sin = _rope_tables(pos, dk // 2)
    grp = lambda a: a.reshape(n_seq // 2, nrow, a.shape[1])
    qk, vd = RET_HEADS * dk, RET_HEADS * dv
    first = prev_states is None
    n_fill = n_layers - 1 if first else 0
    kern = functools.partial(_ret_sample_kernel, t_dec=t_dec, dk=dk, dv=dv, n_fill=n_fill)
    row_spec = lambda w: pl.BlockSpec((1, nrow, w), lambda i: (i, 0, 0))
    st_in = pl.BlockSpec((1, 2, RET_HEADS, dk, dv), lambda i: (layer, i, 0, 0, 0))
    st_out = pl.BlockSpec((1 + n_fill, 2, RET_HEADS, dk, dv), lambda i: (0 if first else layer, i, 0, 0, 0))
    in_specs = [row_spec(qk), row_spec(qk), row_spec(vd), row_spec(vd), _resident(g_gn),
                _resident(cos), _resident(sin), st_in]
    args = [grp(q), grp(k), grp(v), grp(g), g_gn, cos, sin, state_all]
    aliases = {}
    if not first:
        in_specs.append(pl.BlockSpec(memory_space=pl.ANY))
        args.append(prev_states)
        aliases = {len(args) - 1: 1}
    gated, s_new = pl.pallas_call(
        kern, grid=(n_seq // 2,), in_specs=in_specs,
        out_specs=[row_spec(vd), st_out],
        out_shape=[jax.ShapeDtypeStruct((n_seq // 2, nrow, vd), F32),
                   jax.ShapeDtypeStruct(state_all.shape, F32)],
        input_output_aliases=aliases,
        compiler_params=_params("arbitrary"), name="retention_sample",
    )(*args)
    return gated.reshape(n_s, vd), s_new


def _suffix_ones(n):
    r = lax.broadcasted_iota(jnp.int32, (n, n), 0)
    c = lax.broadcasted_iota(jnp.int32, (n, n), 1)
    return jnp.where(r >= c, 1.0, 0.0).astype(BF16)


def _neg_abs(y):
    bits = lax.bitcast_convert_type(y, jnp.int32) | jnp.int32(-2 ** 31)
    return lax.bitcast_convert_type(bits, F32)


def _sb_blocks(ys, ones):
    sps = [jnp.maximum(y, 0.0) + jnp.log(1.0 + jnp.exp2(_neg_abs(y))) * LOG2E for y in ys]
    parts = [_split_bf16(sp) for sp in sps]
    suffixes = [_dot(hi, ones) + _dot(lo, ones) for hi, lo in parts]
    return [(jnp.exp2(y - sfx), sfx[:, :1]) for y, sfx in zip(ys, suffixes)]


def _sb_blocks_skewed(n, logits, ones, use):
    def suffix(y):
        hi, lo = _split_bf16(jnp.maximum(y, 0.0) + jnp.log(1.0 + jnp.exp2(_neg_abs(y))) * LOG2E)
        return _dot(hi, ones) + _dot(lo, ones)

    ys, sfx, results = {}, {}, []
    for step in range(n + 2):
        if step < n:
            ys[step] = logits(step)
        if 1 <= step <= n:
            sfx[step - 1] = suffix(ys[step - 1])
        if step >= 2:
            j = step - 2
            results.append(use(j, jnp.exp2(ys[j] - sfx[j]), sfx[j][:, :1]))
    return results


def _sb_prompt_kernel(bias_ref, q_ref, k_ref, v_ref, o_ref, qh_ref, ones_ref, bm_ref, acc_ref, carry_ref,
                      *, tq, tk, hd, n_pairs):
    grp, qi = pl.program_id(1), pl.program_id(2)
    ratio = tq // tk
    lo_lanes = lax.broadcasted_iota(jnp.int32, (1, LANES), 1) < hd
    row = lax.broadcasted_iota(jnp.int32, (tq, tk), 0)
    col = lax.broadcasted_iota(jnp.int32, (tq, tk), 1)
    for p in range(n_pairs):
        qp = q_ref[:, p * LANES:(p + 1) * LANES]
        zero = jnp.zeros_like(qp)
        qh_ref[2 * p] = jnp.where(lo_lanes, qp, zero)
        qh_ref[2 * p + 1] = jnp.where(lo_lanes, zero, qp)
        for hh in range(2):
            bias = bias_ref[(grp * n_pairs + p) * 2 + hh]
            bm_ref[2 * p + hh, 0] = jnp.full((tq, tk), bias, F32)
            for d in range(ratio):
                bm_ref[2 * p + hh, 1 + d] = jnp.where(col + d * tk < row, bias, MASKED_LOGIT)
    ones_ref[...] = _suffix_ones(tk)
    acc_ref[...] = jnp.zeros_like(acc_ref)
    carry_ref[...] = jnp.zeros_like(carry_ref)
    n_blocks = ratio * qi + ratio

    n_heads = 2 * n_pairs

    def body(it, _):
        ones = ones_ref[...]
        starts, kinds = [], []
        for u in range(SB_BLOCKS_PER_ITER):
            kj = n_blocks - 1 - (it * SB_BLOCKS_PER_ITER + u)
            starts.append(pl.multiple_of(kj * tk, tk))
            kinds.append(jnp.clip(kj - ratio * qi + 1, 0, ratio))

        def logits(i):
            u, h = divmod(i, n_heads)
            kb = k_ref[pl.ds(starts[u], tk), (h // 2) * LANES:(h // 2 + 1) * LANES]
            return _dot_nt(qh_ref[h], kb) + bm_ref[h, kinds[u]]

        def weigh(i, a, tot):
            u, h = divmod(i, n_heads)
            vb = v_ref[pl.ds(starts[u], tk), (h // 2) * LANES:(h // 2 + 1) * LANES]
            vzero = jnp.zeros_like(vb)
            vh = jnp.where(lo_lanes, vb, vzero) if h % 2 == 0 else jnp.where(lo_lanes, vzero, vb)
            return _dot(a.astype(BF16), vh), tot

        blocks = _sb_blocks_skewed(SB_BLOCKS_PER_ITER * n_heads, logits, ones, weigh)
        for p in range(n_pairs):
            c = carry_ref[p]
            acc = acc_ref[:, p * LANES:(p + 1) * LANES]
            for u in range(SB_BLOCKS_PER_ITER):
                (d0, t0), (d1, t1) = blocks[u * n_heads + 2 * p], blocks[u * n_heads + 2 * p + 1]
                acc = acc + jnp.exp2(-c) * (d0 + d1)
                c = c + jnp.where(lo_lanes, t0, t1)
            acc_ref[:, p * LANES:(p + 1) * LANES] = acc
            carry_ref[p] = c
        return 0

    lax.fori_loop(0, n_blocks // SB_BLOCKS_PER_ITER, body, 0)
    o_ref[...] = acc_ref[...].astype(o_ref.dtype)


def _sb_prompt(q, k, v, bias, batch, hd):
    n_p, d = q.shape
    t = n_p // batch
    tq, tk = min(SB_QUERY_BLOCK, t), min(SB_KEY_BLOCK, t)
    n_pairs = SB_HEADS_PER_STEP // 2
    width = n_pairs * LANES
    assert t % tq == 0 and tq % (tk * SB_BLOCKS_PER_ITER) == 0 and 2 * hd == LANES and d % width == 0
    nq = t // tq
    kern = functools.partial(_sb_prompt_kernel, tq=tq, tk=tk, hd=hd, n_pairs=n_pairs)
    return pl.pallas_call(
        kern, grid=(batch, d // width, nq),
        in_specs=[pl.BlockSpec(memory_space=pltpu.SMEM),
                  pl.BlockSpec((tq, width), lambda b, g, i: (b * nq + i, g)),
                  pl.BlockSpec((t, width), lambda b, g, i: (b, g)),
                  pl.BlockSpec((t, width), lambda b, g, i: (b, g))],
        out_specs=pl.BlockSpec((tq, width), lambda b, g, i: (b * nq + i, g)),
        out_shape=jax.ShapeDtypeStruct((n_p, d), BF16),
        scratch_shapes=[pltpu.VMEM((2 * n_pairs, tq, LANES), BF16), pltpu.VMEM((tk, tk), BF16),
                        pltpu.VMEM((2 * n_pairs, 1 + tq // tk, tq, tk), F32), pltpu.VMEM((tq, width), F32),
                        pltpu.VMEM((n_pairs, tq, LANES), F32)],
        compiler_params=_params("arbitrary", "arbitrary", "arbitrary"), name="sb_prompt",
    )(bias, q, k, v)


def _sb_sample_kernel(pt_ref, q_ref, kn_ref, vn_ref, bias_ref, *refs, t_dec, hd, n_pages):
    del pt_ref
    kt_refs, vt_refs, o_ref = refs[:n_pages], refs[n_pages:2 * n_pages], refs[2 * n_pages]
    page = kt_refs[0].shape[2]
    nrow = t_dec * SB_HEADS
    d = SB_HEADS * hd
    blk = 2 * page
    head_mask = ((lax.broadcasted_iota(jnp.int32, (SB_HEADS, d), 1) // hd)
                 == lax.broadcasted_iota(jnp.int32, (SB_HEADS, d), 0))
    bias = bias_ref[:, :1]
    qv = q_ref[0]
    qbd = jnp.concatenate([jnp.where(head_mask, qv[t:t + 1, :], 0.0) for t in range(t_dec)]
                          + [jnp.zeros((LANES - nrow, d), F32)], axis=0).astype(BF16)
    key_i = lax.broadcasted_iota(jnp.int32, (LANES, blk), 1)
    qry_t = lax.broadcasted_iota(jnp.int32, (LANES, blk), 0) // SB_HEADS
    y_new = _dot_nt(qbd, _pad_rows(kn_ref[0], blk).astype(BF16)) + bias
    y_pages = [_dot(qbd, kt[0].astype(BF16)) for kt in kt_refs]
    ys = [jnp.where(key_i < qry_t, y_new, MASKED_LOGIT)]
    ys += [jnp.concatenate([y_pages[2 * i], y_pages[2 * i + 1]], axis=1) + bias for i in reversed(range(n_pages // 2))]
    blocks = _sb_blocks(ys, _suffix_ones(blk))
    carry = jnp.zeros((LANES, 1), F32)
    weights = []
    for a, tot in blocks:
        weights.append((a * jnp.exp2(-carry)).astype(BF16))
        carry = carry + tot
    out_new = _dot(weights[0], _pad_rows(vn_ref[0], blk).astype(BF16))
    out_t = None
    for w, i in zip(weights[1:], reversed(range(n_pages // 2))):
        v_t = jnp.concatenate([vt_refs[2 * i][0], vt_refs[2 * i + 1][0]], axis=1).astype(BF16)
        part = _dot_nt(v_t, w)
        out_t = part if out_t is None else out_t + part
    rows = [jnp.sum(jnp.where(head_mask, out_new[t * SB_HEADS:(t + 1) * SB_HEADS, :], 0.0), axis=0, keepdims=True)
            for t in range(t_dec)]
    f_head = lax.broadcasted_iota(jnp.int32, (d, LANES), 0) // hd
    r_idx = lax.broadcasted_iota(jnp.int32, (d, LANES), 1)
    hi, lo = _split_bf16(jnp.where((r_idx % SB_HEADS == f_head) & (r_idx < nrow), out_t, 0.0))
    sel = jnp.where(lax.broadcasted_iota(jnp.int32, (16, LANES), 1) // SB_HEADS
                    == lax.broadcasted_iota(jnp.int32, (16, LANES), 0), 1.0, 0.0).astype(BF16)
    past = _dot_nt(sel, hi) + _dot_nt(sel, lo)
    o_ref[0] = jnp.concatenate(rows, axis=0) + past[:t_dec]


def _feature_major_pages(cache):
    n_pool, page, heads, hd = cache.shape
    return jnp.transpose(cache, (0, 2, 3, 1)).reshape(n_pool, heads * hd, page)


def _sb_sample(q, k_new, v_new, pages_kt, pages_vt, page_table, bias, t_dec, hd):
    n_s, d = q.shape
    n_seq = n_s // t_dec
    page = pages_kt.shape[2]
    n_pages = page_table.shape[1]
    nrow = t_dec * SB_HEADS
    assert page == LANES and nrow <= LANES and n_pages % 2 == 0 and t_dec <= 16
    bias_rows = jnp.broadcast_to(jnp.tile(bias.astype(F32), t_dec)[:, None], (nrow, LANES))
    bias_rows = jnp.concatenate([bias_rows, jnp.zeros((LANES - nrow, LANES), F32)], axis=0)
    seq3 = lambda a: a.reshape(n_seq, t_dec, d)
    seq_spec = pl.BlockSpec((1, t_dec, d), lambda b, pt: (b, 0, 0))

    def page_spec(j):
        return pl.BlockSpec((1, d, page), lambda b, pt: (pt[b * n_pages + j], 0, 0))

    kern = functools.partial(_sb_sample_kernel, t_dec=t_dec, hd=hd, n_pages=n_pages)
    out = pl.pallas_call(
        kern,
        grid_spec=pltpu.PrefetchScalarGridSpec(
            num_scalar_prefetch=1, grid=(n_seq,),
            in_specs=[seq_spec, seq_spec, seq_spec, pl.BlockSpec((LANES, LANES), lambda b, pt: (0, 0))]
                     + [page_spec(j) for j in range(n_pages)] + [page_spec(j) for j in range(n_pages)],
            out_specs=seq_spec),
        out_shape=jax.ShapeDtypeStruct((n_seq, t_dec, d), F32),
        compiler_params=_params("arbitrary"), name="sb_sample",
    )(page_table.reshape(-1), seq3(q), seq3(k_new), seq3(v_new), bias_rows,
      *([pages_kt] * n_pages), *([pages_vt] * n_pages))
    return out.reshape(n_s, d)


def kernel(x_prompt, x_sample, p_prompt, p_sample, state_ret, cache_k, cache_v, page_table, g_norm, w_ff1_gate, w_ff1_up, w_ff1_down, w_ff2_gate, w_ff2_up, w_ff2_down, w_ple_up, w_ple_gate, g_ple, w_ret_in, w_ret_out, g_ret_gn, g_kv, w_kv, w_sb_q, w_sb_out, b_sb, g_final):
    batch, t_prompt, d = x_prompt.shape
    n_dec, t_dec, _ = x_sample.shape
    depth = g_norm.shape[0]
    n_ret = w_ret_in.shape[0]
    dk = d // RET_HEADS
    dv = w_ret_out.shape[1] // RET_HEADS
    hd = d // SB_HEADS
    past_len = page_table.shape[1] * cache_k.shape[1]
    row = lambda a: a.reshape(1, -1)
    bf = lambda a: a.astype(BF16)

    xp = x_prompt.reshape(batch * t_prompt, d)
    xs = x_sample.reshape(n_dec * t_dec, d)
    x_out = [(d, F32)]
    states_p, states_s = [], None
    kv = None
    pages_kt, pages_vt = _feature_major_pages(cache_k), _feature_major_pages(cache_v)
    for i in range(depth):
        (xp, xs), = _tokenwise(_ffn_body, [(xp, xs)],
                               [row(g_norm[i, 0]), bf(w_ff1_gate[i]), bf(w_ff1_up[i]), bf(w_ff1_down[i])],
                               x_out, "ffn1")
        if i < n_ret:
            g_gn = row(g_ret_gn[i])
            xp, s_p = _retention_prompt(xp, batch, row(g_norm[i, 1]), bf(w_ret_in[i]), bf(w_ret_out[i]),
                                        g_gn, dk, dv)
            q, k, v, g = _rowwise(functools.partial(_ret_in_proj_body, qk=RET_HEADS * dk, vd=RET_HEADS * dv),
                                  [xs], [row(g_norm[i, 1]), bf(w_ret_in[i])],
                                  [(RET_HEADS * dk, F32), (RET_HEADS * dk, F32),
                                   (RET_HEADS * dv, F32), (RET_HEADS * dv, F32)], "ret_in_proj_sample")
            gated, states_s = _retention_sample(q, k, v, g, g_gn, state_ret, i, states_s, t_dec, past_len, dk, dv)
            xs, = _rowwise(_proj_residual_body, [xs, gated], [bf(w_ret_out[i])], x_out, "ret_out_proj_sample")
            states_p.append(s_p)
        else:
            l = i - n_ret
            (qp, qs), = _tokenwise(functools.partial(_sb_q_body, scale=LOG2E * hd ** -0.5), [(xp, xs)],
                                   [row(g_norm[i, 1]), bf(w_sb_q[l])], [(d, BF16)], "sb_q_proj")
            (kp32, ks32), (vp32, vs32), (kp16, _), (vp16, _) = kv
            bias2 = b_sb[l].astype(F32) * LOG2E
            o_p = _sb_prompt(qp, kp16, vp16, bias2, batch, hd)
            o_s = _sb_sample(qs.astype(F32), ks32, vs32, pages_kt, pages_vt, page_table, bias2, t_dec, hd)
            (xp, xs), = _tokenwise(_proj_residual_body, [(xp, xs), (o_p, bf(o_s))], [bf(w_sb_out[l])],
                                   x_out, "sb_out_proj")
        (xp, xs), = _tokenwise(_ffn_body, [(xp, xs)],
                               [row(g_norm[i, 2]), bf(w_ff2_gate[i]), bf(w_ff2_up[i]), bf(w_ff2_down[i])],
                               x_out, "ffn2")
        p_p = p_prompt[i].reshape(batch * t_prompt, -1)
        p_s = p_sample[i].reshape(n_dec * t_dec, -1)
        (xp, xs), = _tokenwise(_ple_body, [(xp, xs), (p_p, p_s)],
                               [row(g_ple[i]), bf(w_ple_up[i]), bf(w_ple_gate[i])], x_out, "ple")
        if i == n_ret - 1:
            kv = _tokenwise(_kv_body, [(xp, xs)], [row(g_kv), bf(w_kv)],
                            [(d, F32), (d, F32), (d, BF16), (d, BF16)], "kv_proj")
    (yp, ys), = _tokenwise(_final_norm_body, [(xp, xs)], [row(g_final)], x_out, "final_norm")
    (kp32, ks32), (vp32, vs32) = kv[0], kv[1]
    return (yp.reshape(batch, t_prompt, d), ys.reshape(n_dec, t_dec, d),
            jnp.stack(states_p), states_s,
            kp32.reshape(batch, t_prompt, SB_HEADS, hd), vp32.reshape(batch, t_prompt, SB_HEADS, hd),
            ks32.reshape(n_dec, t_dec, SB_HEADS, hd), vs32.reshape(n_dec, t_dec, SB_HEADS, hd))
```

```python
import functools
import math

import jax
import jax.numpy as jnp
from jax import lax
from jax.experimental import pallas as pl
from jax.experimental.pallas import tpu as pltpu

F32 = jnp.float32
BF16 = jnp.bfloat16

EPS = 1e-6
ROPE_BASE = 10000.0
RET_HEADS = 4
SB_HEADS = 16
LANES = 128
VMEM_LIMIT_BYTES = 58 * 1024 * 1024
MASKED_LOGIT = -1e30
LOG2E = 1.4426950408889634
TOKEN_TILE = 512
RET_CHUNK_ROWS = 512
SB_QUERY_BLOCK = 512
SB_KEY_BLOCK = 256
SB_HEADS_PER_STEP = 4
SB_BLOCKS_PER_VISIT = 2


def _params(*sem, flags=None):
    return pltpu.CompilerParams(dimension_semantics=sem, vmem_limit_bytes=VMEM_LIMIT_BYTES, flags=flags)


def _dot(a, b):
    return jnp.dot(a, b, preferred_element_type=F32)


def _dot_nt(a, b):
    return lax.dot_general(a, b, (((1,), (1,)), ((), ())), preferred_element_type=F32)


def _rmsnorm(x, g):
    y = x * lax.rsqrt(jnp.mean(x * x, axis=-1, keepdims=True) + EPS)
    return y * g


def _silu(x):
    return x * jax.nn.sigmoid(x)


def _softplus2(y):
    neg_abs = lax.bitcast_convert_type(lax.bitcast_convert_type(y, jnp.int32) | jnp.int32(-2 ** 31), F32)
    return jnp.maximum(y, 0.0) + jnp.log(1.0 + jnp.exp2(neg_abs)) * LOG2E


def _split_bf16(x):
    hi = x.astype(BF16)
    lo = (x - hi.astype(F32)).astype(BF16)
    return hi, lo


def _resident(arr):
    nd = arr.ndim
    return pl.BlockSpec(arr.shape, lambda *_: (0,) * nd, pipeline_mode=pl.Buffered(1))


def _tokenwise(body, streams, consts, outs, name):
    n_p, n_s = streams[0][0].shape[0], streams[0][1].shape[0]
    tm = min(TOKEN_TILE, n_s)
    assert n_p % tm == 0 and n_s % tm == 0
    p_tiles, s_tiles = n_p // tm, n_s // tm
    ns, nc, no = len(streams), len(consts), len(outs)

    def kern(*refs):
        in_refs, c_refs, o_refs = refs[:2 * ns], refs[2 * ns:2 * ns + nc], refs[2 * ns + nc:]
        is_prompt = pl.program_id(0) < p_tiles
        vals = [jnp.where(is_prompt, in_refs[2 * k][...], in_refs[2 * k + 1][...]) for k in range(ns)]
        res = body(*vals, *c_refs)

        @pl.when(is_prompt)
        def _():
            for k in range(no):
                o_refs[2 * k][...] = res[k].astype(outs[k][1])

        @pl.when(jnp.logical_not(is_prompt))
        def _():
            for k in range(no):
                o_refs[2 * k + 1][...] = res[k].astype(outs[k][1])

    def p_spec(w):
        return pl.BlockSpec((tm, w), lambda i: (jnp.minimum(i, p_tiles - 1), 0))

    def s_spec(w):
        return pl.BlockSpec((tm, w), lambda i: (jnp.maximum(i - p_tiles, 0), 0))

    in_specs, args = [], []
    for a_p, a_s in streams:
        in_specs += [p_spec(a_p.shape[1]), s_spec(a_s.shape[1])]
        args += [a_p, a_s]
    in_specs += [_resident(c) for c in consts]
    out_specs, out_shape = [], []
    for w, dt in outs:
        out_specs += [p_spec(w), s_spec(w)]
        out_shape += [jax.ShapeDtypeStruct((n_p, w), dt), jax.ShapeDtypeStruct((n_s, w), dt)]
    res = pl.pallas_call(
        kern, grid=(p_tiles + s_tiles,), in_specs=in_specs, out_specs=out_specs,
        out_shape=out_shape, compiler_params=_params("arbitrary"), name=name,
    )(*args, *consts)
    return [(res[2 * k], res[2 * k + 1]) for k in range(no)]


def _rowwise(body, rows, consts, outs, name):
    n = rows[0].shape[0]
    tm = min(TOKEN_TILE, n)
    assert n % tm == 0
    nr, nc, no = len(rows), len(consts), len(outs)

    def kern(*refs):
        res = body(*[r[...] for r in refs[:nr]], *refs[nr:nr + nc])
        for k in range(no):
            refs[nr + nc + k][...] = res[k].astype(outs[k][1])

    in_specs = [pl.BlockSpec((tm, a.shape[1]), lambda i: (i, 0)) for a in rows]
    in_specs += [_resident(c) for c in consts]
    return pl.pallas_call(
        kern, grid=(n // tm,), in_specs=in_specs,
        out_specs=[pl.BlockSpec((tm, w), lambda i: (i, 0)) for w, _ in outs],
        out_shape=[jax.ShapeDtypeStruct((n, w), dt) for w, dt in outs],
        compiler_params=_params("arbitrary"), name=name,
    )(*rows, *consts)


def _ffn_body(x, g_ref, wg_ref, wu_ref, wd_ref):
    h = _rmsnorm(x, g_ref[...]).astype(BF16)
    d_ff = wg_ref.shape[1]
    chunk = 256 if d_ff % 256 == 0 else LANES
    acc = None
    for c0 in range(0, d_ff, chunk):
        gate = _dot(h, wg_ref[:, c0:c0 + chunk])
        up = _dot(h, wu_ref[:, c0:c0 + chunk])
        d = _dot((_silu(gate) * up).astype(BF16), wd_ref[c0:c0 + chunk, :])
        acc = d if acc is None else acc + d
    return (x + 0.5 * acc,)


def _ple_body(x, p, g_ref, wup_ref, wgate_ref):
    h = _rmsnorm(x, g_ref[...]).astype(BF16)
    gate = jax.nn.sigmoid(_dot(h, wgate_ref[...]))
    return (x + _dot(p.astype(BF16), wup_ref[...]) * gate,)


def _sb_q_body(x, g_ref, w_ref, *, scale):
    return (_dot(_rmsnorm(x, g_ref[...]).astype(BF16), w_ref[...]) * scale,)


def _kv_body(x, g_ref, w_ref):
    h = _rmsnorm(x, g_ref[...]).astype(BF16)
    half = w_ref.shape[1] // 2
    return (_dot(h, w_ref[:, :half]), _dot(h, w_ref[:, half:]))


def _kv_prompt_kernel(x_ref, g_ref, w_ref, kt_ref, vt_ref, k16_ref, v16_ref):
    k, v = _kv_body(x_ref[...], g_ref, w_ref)
    kt_ref[0] = k.T
    vt_ref[0] = v.T
    k16_ref[...] = k.astype(BF16)
    v16_ref[...] = v.astype(BF16)


def _kv_prompt(xp, batch, g, w):
    n_p, d = xp.shape
    t = n_p // batch
    tm = min(TOKEN_TILE, t)
    assert t % tm == 0
    nt = t // tm
    row_spec = pl.BlockSpec((tm, d), lambda b, i: (b * nt + i, 0))
    fm_spec = pl.BlockSpec((1, d, tm), lambda b, i: (b, 0, i))
    return pl.pallas_call(
        _kv_prompt_kernel, grid=(batch, nt),
        in_specs=[row_spec, _resident(g), _resident(w)],
        out_specs=[fm_spec, fm_spec, row_spec, row_spec],
        out_shape=[jax.ShapeDtypeStruct((batch, d, t), F32), jax.ShapeDtypeStruct((batch, d, t), F32),
                   jax.ShapeDtypeStruct((n_p, d), BF16), jax.ShapeDtypeStruct((n_p, d), BF16)],
        compiler_params=_params("arbitrary", "arbitrary"), name="kv_proj_prompt",
    )(xp, g, w)


def _proj_residual_body(x, a, w_ref):
    return (x + _dot(a.astype(BF16), w_ref[...]),)


def _ffn_ple_body(x, p, g_ref, wg_ref, wu_ref, wd_ref, gp_ref, wup_ref, wgate_ref, *final_ref, final):
    (x,) = _ffn_body(x, g_ref, wg_ref, wu_ref, wd_ref)
    (x,) = _ple_body(x, p, gp_ref, wup_ref, wgate_ref)
    return (_rmsnorm(x, final_ref[0][...]) if final else x,)


def _ret_in_proj_body(x, g_ref, w_ref, *, qk, vd):
    h = _rmsnorm(x, g_ref[...]).astype(BF16)
    return (_dot(h, w_ref[:, :qk]), _dot(h, w_ref[:, qk:2 * qk]),
            _dot(h, w_ref[:, 2 * qk:2 * qk + vd]), _dot(h, w_ref[:, 2 * qk + vd:]))


def _rope_tables(pos, half):
    inv_freq = ROPE_BASE ** (-jnp.arange(half, dtype=F32) / half)
    ang = pos.astype(F32)[:, None] * inv_freq[None, :]
    return jnp.cos(ang), jnp.sin(ang)


def _rope(x, cos, sin):
    half = x.shape[-1] // 2
    x1, x2 = x[:, :half], x[:, half:]
    return jnp.concatenate([x1 * cos - x2 * sin, x2 * cos + x1 * sin], axis=-1)


def _log_decay(head):
    return math.log(1.0 - 2.0 ** (-5.0 - head))


def _group_norm_gate(o, g, ggn):
    mu = jnp.mean(o, axis=-1, keepdims=True)
    var = jnp.mean(jnp.square(o - mu), axis=-1, keepdims=True)
    on = (o - mu) * lax.rsqrt(var + EPS) * ggn
    return _silu(g) * on


def _ret_prompt_kernel(x_ref, gn_ref, win_ref, wout_ref, ggn_ref, cos_ref, sin_ref,
                       o_ref, s_ref, *, rows, dk, dv):
    qk, vd = RET_HEADS * dk, RET_HEADS * dv

    @pl.when(pl.program_id(1) == 0)
    def _():
        s_ref[...] = jnp.zeros_like(s_ref)

    x = x_ref[...]
    h = _rmsnorm(x, gn_ref[...]).astype(BF16)
    cos, sin = cos_ref[...], sin_ref[...]
    diff = (lax.broadcasted_iota(jnp.int32, (rows, rows), 0)
            - lax.broadcasted_iota(jnp.int32, (rows, rows), 1)).astype(F32)
    ri = lax.broadcasted_iota(jnp.int32, (rows, 1), 0).astype(F32)
    acc = None
    for hh in range(RET_HEADS):
        lg = _log_decay(hh)
        q = _rope(_dot(h, win_ref[:, hh * dk:(hh + 1) * dk]), cos, sin)
        k = _rope(_dot(h, win_ref[:, qk + hh * dk:qk + (hh + 1) * dk]), cos, sin) * (dk ** -0.5)
        v = _dot(h, win_ref[:, 2 * qk + hh * dv:2 * qk + (hh + 1) * dv]).astype(BF16)
        g = _dot(h, win_ref[:, 2 * qk + vd + hh * dv:2 * qk + vd + (hh + 1) * dv])
        decay = jnp.where(diff >= 0, jnp.exp(jnp.maximum(diff, 0.0) * lg), 0.0)
        s = _dot_nt(q.astype(BF16), k.astype(BF16)) * decay
        state = s_ref[0, hh]
        q_dec = (q * jnp.exp((ri + 1.0) * lg)).astype(BF16)
        o = _dot(s.astype(BF16), v) + _dot(q_dec, state.astype(BF16))
        k_dec = k * jnp.exp((rows - 1.0 - ri) * lg)
        s_ref[0, hh] = math.exp(rows * lg) * state + _dot(k_dec.T.astype(BF16), v)
        gated = _group_norm_gate(o, g, ggn_ref[:, hh * dv:(hh + 1) * dv]).astype(BF16)
        d = _dot(gated, wout_ref[hh * dv:(hh + 1) * dv, :])
        acc = d if acc is None else acc + d
    o_ref[...] = x + acc


def _retention_prompt(xp, batch, g, w_in, w_out, g_gn, dk, dv):
    n_p, d = xp.shape
    t = n_p // batch
    rows = min(RET_CHUNK_ROWS, t)
    assert t % rows == 0
    nch = t // rows
    cos, sin = _rope_tables(jnp.arange(t, dtype=jnp.int32), dk // 2)
    kern = functools.partial(_ret_prompt_kernel, rows=rows, dk=dk, dv=dv)
    return pl.pallas_call(
        kern, grid=(batch, nch),
        in_specs=[pl.BlockSpec((rows, d), lambda b, c: (b * nch + c, 0)),
                  _resident(g), _resident(w_in), _resident(w_out), _resident(g_gn),
                  pl.BlockSpec((rows, dk // 2), lambda b, c: (c, 0)),
                  pl.BlockSpec((rows, dk // 2), lambda b, c: (c, 0))],
        out_specs=[pl.BlockSpec((rows, d), lambda b, c: (b * nch + c, 0)),
                   pl.BlockSpec((1, RET_HEADS, dk, dv), lambda b, c: (b, 0, 0, 0))],
        out_shape=[jax.ShapeDtypeStruct((n_p, d), F32),
                   jax.ShapeDtypeStruct((batch, RET_HEADS, dk, dv), F32)],
        compiler_params=_params("arbitrary", "arbitrary"), name="retention_prompt",
    )(xp, g, w_in, w_out, g_gn, cos, sin)


def _pad_rows(x, rows):
    return jnp.concatenate([x, jnp.zeros((rows - x.shape[0], x.shape[1]), x.dtype)], axis=0)


def _ret_sample_kernel(q_ref, k_ref, v_ref, g_ref, ggn_ref, cos_ref, sin_ref, s_ref, *rest,
                       t_dec, dk, dv, n_fill):
    o_ref, so_ref = rest[-2], rest[-1]
    nrow = 2 * t_dec
    cos, sin = cos_ref[...], sin_ref[...]
    r_i = lax.broadcasted_iota(jnp.int32, (nrow, LANES), 0)
    c_i = lax.broadcasted_iota(jnp.int32, (nrow, LANES), 1)
    same_seq = (r_i // t_dec) == (c_i // t_dec)
    diff = ((r_i % t_dec) - (c_i % t_dec)).astype(F32)
    valid = same_seq & (diff >= 0) & (c_i < nrow)
    t_row = (lax.broadcasted_iota(jnp.int32, (nrow, 1), 0) % t_dec).astype(F32)
    seq_row = lax.broadcasted_iota(jnp.int32, (nrow, 1), 0) // t_dec
    seq_lane = lax.broadcasted_iota(jnp.int32, (1, LANES), 1) // t_dec
    for fill in range(n_fill):
        so_ref[1 + fill] = jnp.zeros(so_ref.shape[1:], so_ref.dtype)
    outs = []
    for hh in range(RET_HEADS):
        lg = _log_decay(hh)
        q = _rope(q_ref[0, :, hh * dk:(hh + 1) * dk], cos, sin)
        k = _rope(k_ref[0, :, hh * dk:(hh + 1) * dk], cos, sin) * (dk ** -0.5)
        v = _pad_rows(v_ref[0, :, hh * dv:(hh + 1) * dv], LANES).astype(BF16)
        g = g_ref[0, :, hh * dv:(hh + 1) * dv]
        decay = jnp.where(valid, jnp.exp(jnp.maximum(diff, 0.0) * lg), 0.0)
        q16 = _pad_rows(q, 16)
        s = _dot_nt(q16.astype(BF16), _pad_rows(k, LANES).astype(BF16))[:nrow] * decay
        o = _dot(_pad_rows(s, 16).astype(BF16), v)[:nrow]
        q_dec = _pad_rows(q * jnp.exp((t_row + 1.0) * lg), 16).astype(BF16)
        k_dec_t = _pad_rows(k * jnp.exp((t_dec - 1.0 - t_row) * lg), LANES).T
        for sq in range(2):
            state = s_ref[0, sq, hh]
            o_state = _dot(q_dec, state.astype(BF16))[:nrow]
            o = o + jnp.where(seq_row == sq, o_state, 0.0)
            k_sq = jnp.where(seq_lane == sq, k_dec_t, 0.0).astype(BF16)
            so_ref[0, sq, hh] = math.exp(t_dec * lg) * state + _dot(k_sq, v)
        outs.append(_group_norm_gate(o, g, ggn_ref[:, hh * dv:(hh + 1) * dv]))
    o_ref[0] = jnp.concatenate(outs, axis=-1)


def _retention_sample(q, k, v, g, g_gn, state_all, layer, prev_states, t_dec, past_len, dk, dv):
    n_s = q.shape[0]
    n_seq = n_s // t_dec
    n_layers = state_all.shape[0]
    assert n_seq % 2 == 0
    nrow = 2 * t_dec
    pos = past_len + (jnp.arange(nrow, dtype=jnp.int32) % t_dec)
    cos, sin = _rope_tables(pos, dk // 2)
    grp = lambda a: a.reshape(n_seq // 2, nrow, a.shape[1])
    qk, vd = RET_HEADS * dk, RET_HEADS * dv
    first = prev_states is None
    n_fill = n_layers - 1 if first else 0
    kern = functools.partial(_ret_sample_kernel, t_dec=t_dec, dk=dk, dv=dv, n_fill=n_fill)
    row_spec = lambda w: pl.BlockSpec((1, nrow, w), lambda i: (i, 0, 0))
    st_in = pl.BlockSpec((1, 2, RET_HEADS, dk, dv), lambda i: (layer, i, 0, 0, 0))
    st_out = pl.BlockSpec((1 + n_fill, 2, RET_HEADS, dk, dv), lambda i: (0 if first else layer, i, 0, 0, 0))
    in_specs = [row_spec(qk), row_spec(qk), row_spec(vd), row_spec(vd), _resident(g_gn),
                _resident(cos), _resident(sin), st_in]
    args = [grp(q), grp(k), grp(v), grp(g), g_gn, cos, sin, state_all]
    aliases = {}
    if not first:
        in_specs.append(pl.BlockSpec(memory_space=pl.ANY))
        args.append(prev_states)
        aliases = {len(args) - 1: 1}
    gated, s_new = pl.pallas_call(
        kern, grid=(n_seq // 2,), in_specs=in_specs,
        out_specs=[row_spec(vd), st_out],
        out_shape=[jax.ShapeDtypeStruct((n_seq // 2, nrow, vd), F32),
                   jax.ShapeDtypeStruct(state_all.shape, F32)],
        input_output_aliases=aliases,
        compiler_params=_params("arbitrary"), name="retention_sample",
    )(*args)
    return gated.reshape(n_s, vd), s_new


def _suffix_ones(n):
    r = lax.broadcasted_iota(jnp.int32, (n, n), 0)
    c = lax.broadcasted_iota(jnp.int32, (n, n), 1)
    return jnp.where(r >= c, 1.0, 0.0).astype(BF16)


def _sb_blocks(ys, ones):
    parts = [_split_bf16(_softplus2(y)) for y in ys]
    suffixes = [_dot(hi, ones) + _dot(lo, ones) for hi, lo in parts]
    return [(jnp.exp2(y - sfx), sfx[:, :1]) for y, sfx in zip(ys, suffixes)]


def _sb_blocks_skewed(n, logits, ones, use):
    def suffix(y):
        hi, lo = _split_bf16(_softplus2(y))
        return _dot(hi, ones) + _dot(lo, ones)

    ys, sfx, results = {}, {}, []
    for step in range(n + 2):
        if step < n:
            ys[step] = logits(step)
        if 1 <= step <= n:
            sfx[step - 1] = suffix(ys[step - 1])
        if step >= 2:
            j = step - 2
            results.append(use(j, jnp.exp2(ys[j] - sfx[j]), sfx[j][:, :1]))
    return results


def _sb_prompt_kernel(bias_ref, q_ref, k_ref, v_ref, o_ref, qh_ref, ones_ref, bm_ref, acc_ref, carry_ref,
                      *, tq, tk, hd, n_pairs):
    grp, qi = pl.program_id(1), pl.program_id(2)
    ratio = tq // tk
    lo_lanes = lax.broadcasted_iota(jnp.int32, (1, LANES), 1) < hd
    row = lax.broadcasted_iota(jnp.int32, (tq, tk), 0)
    col = lax.broadcasted_iota(jnp.int32, (tq, tk), 1)
    for p in range(n_pairs):
        qp = q_ref[:, p * LANES:(p + 1) * LANES]
        zero = jnp.zeros_like(qp)
        qh_ref[2 * p] = jnp.where(lo_lanes, qp, zero)
        qh_ref[2 * p + 1] = jnp.where(lo_lanes, zero, qp)
        for hh in range(2):
            bias = bias_ref[(grp * n_pairs + p) * 2 + hh]
            bm_ref[2 * p + hh, 0] = jnp.full((tq, tk), bias, F32)
            for d in range(ratio):
                bm_ref[2 * p + hh, 1 + d] = jnp.where(col + d * tk < row, bias, MASKED_LOGIT)
    ones_ref[...] = _suffix_ones(tk)
    acc_ref[...] = jnp.zeros_like(acc_ref)
    carry_ref[...] = jnp.zeros_like(carry_ref)
    n_blocks = ratio * qi + ratio
    n_heads = 2 * n_pairs

    def visit(top, count):
        ones = ones_ref[...]
        starts = [pl.multiple_of((top - u) * tk, tk) for u in range(count)]
        kinds = [jnp.clip(top - u - ratio * qi + 1, 0, ratio) for u in range(count)]

        def logits(i):
            u, h = divmod(i, n_heads)
            kb = k_ref[pl.ds(starts[u], tk), (h // 2) * LANES:(h // 2 + 1) * LANES]
            return _dot_nt(qh_ref[h], kb) + bm_ref[h, kinds[u]]

        def weigh(i, a, tot):
            u, h = divmod(i, n_heads)
            vb = v_ref[pl.ds(starts[u], tk), (h // 2) * LANES:(h // 2 + 1) * LANES]
            vzero = jnp.zeros_like(vb)
            vh = jnp.where(lo_lanes, vb, vzero) if h % 2 == 0 else jnp.where(lo_lanes, vzero, vb)
            return _dot(a.astype(BF16), vh), tot

        blocks = _sb_blocks_skewed(count * n_heads, logits, ones, weigh)
        for p in range(n_pairs):
            c = carry_ref[p]
            acc = acc_ref[:, p * LANES:(p + 1) * LANES]
            for u in range(count):
                (d0, t0), (d1, t1) = blocks[u * n_heads + 2 * p], blocks[u * n_heads + 2 * p + 1]
                acc = acc + jnp.exp2(-c) * (d0 + d1)
                c = c + jnp.where(lo_lanes, t0, t1)
            acc_ref[:, p * LANES:(p + 1) * LANES] = acc
            carry_ref[p] = c

    big = SB_BLOCKS_PER_VISIT
    rem = lax.rem(n_blocks, big)
    if big > ratio:
        @pl.when(rem != 0)
        def _():
            visit(n_blocks - 1, ratio)

    def body(it, _):
        visit(n_blocks - 1 - rem - it * big, big)
        return 0

    lax.fori_loop(0, n_blocks // big, body, 0)
    o_ref[...] = acc_ref[...].astype(o_ref.dtype)


def _sb_prompt(q, k, v, bias, batch, hd):
    n_p, d = q.shape
    t = n_p // batch
    tq, tk = min(SB_QUERY_BLOCK, t), min(SB_KEY_BLOCK, t)
    n_pairs = SB_HEADS_PER_STEP // 2
    width = n_pairs * LANES
    assert t % tq == 0 and tq % tk == 0 and 2 * hd == LANES and d % width == 0
    assert SB_BLOCKS_PER_VISIT % (tq // tk) == 0 and SB_BLOCKS_PER_VISIT <= 2 * (tq // tk)
    nq = t // tq
    kern = functools.partial(_sb_prompt_kernel, tq=tq, tk=tk, hd=hd, n_pairs=n_pairs)
    return pl.pallas_call(
        kern, grid=(batch, d // width, nq),
        in_specs=[pl.BlockSpec(memory_space=pltpu.SMEM),
                  pl.BlockSpec((tq, width), lambda b, g, i: (b * nq + i, g)),
                  pl.BlockSpec((t, width), lambda b, g, i: (b, g)),
                  pl.BlockSpec((t, width), lambda b, g, i: (b, g))],
        out_specs=pl.BlockSpec((tq, width), lambda b, g, i: (b * nq + i, g)),
        out_shape=jax.ShapeDtypeStruct((n_p, d), BF16),
        scratch_shapes=[pltpu.VMEM((2 * n_pairs, tq, LANES), BF16), pltpu.VMEM((tk, tk), BF16),
                        pltpu.VMEM((2 * n_pairs, 1 + tq // tk, tq, tk), F32), pltpu.VMEM((tq, width), F32),
                        pltpu.VMEM((n_pairs, tq, LANES), F32)],
        compiler_params=_params("arbitrary", "arbitrary", "arbitrary"), name="sb_prompt",
    )(bias, q, k, v)


def _sb_sample_kernel(pt_ref, q_ref, kn_ref, vn_ref, bias_ref, *refs, t_dec, hd, n_pages):
    del pt_ref
    kt_refs, vt_refs, o_ref = refs[:n_pages], refs[n_pages:2 * n_pages], refs[2 * n_pages]
    page = kt_refs[0].shape[2]
    nrow = t_dec * SB_HEADS
    d = SB_HEADS * hd
    blk = 2 * page
    head_mask = ((lax.broadcasted_iota(jnp.int32, (SB_HEADS, d), 1) // hd)
                 == lax.broadcasted_iota(jnp.int32, (SB_HEADS, d), 0))
    bias = bias_ref[:, :1]
    qv = q_ref[0]
    qbd = jnp.concatenate([jnp.where(head_mask, qv[t:t + 1, :], 0.0) for t in range(t_dec)]
                          + [jnp.zeros((LANES - nrow, d), F32)], axis=0).astype(BF16)
    key_i = lax.broadcasted_iota(jnp.int32, (LANES, blk), 1)
    qry_t = lax.broadcasted_iota(jnp.int32, (LANES, blk), 0) // SB_HEADS
    y_new = _dot_nt(qbd, _pad_rows(kn_ref[0], blk).astype(BF16)) + bias
    y_pages = [_dot(qbd, kt[0].astype(BF16)) for kt in kt_refs]
    ys = [jnp.where(key_i < qry_t, y_new, MASKED_LOGIT)]
    ys += [jnp.concatenate([y_pages[2 * i], y_pages[2 * i + 1]], axis=1) + bias for i in reversed(range(n_pages // 2))]
    blocks = _sb_blocks(ys, _suffix_ones(blk))
    carry = jnp.zeros((LANES, 1), F32)
    weights = []
    for a, tot in blocks:
        weights.append((a * jnp.exp2(-carry)).astype(BF16))
        carry = carry + tot
    out_new = _dot(weights[0], _pad_rows(vn_ref[0], blk).astype(BF16))
    out_t = None
    for w, i in zip(weights[1:], reversed(range(n_pages // 2))):
        v_t = jnp.concatenate([vt_refs[2 * i][0], vt_refs[2 * i + 1][0]], axis=1).astype(BF16)
        part = _dot_nt(v_t, w)
        out_t = part if out_t is None else out_t + part
    rows = [jnp.sum(jnp.where(head_mask, out_new[t * SB_HEADS:(t + 1) * SB_HEADS, :], 0.0), axis=0, keepdims=True)
            for t in range(t_dec)]
    f_head = lax.broadcasted_iota(jnp.int32, (d, LANES), 0) // hd
    r_idx = lax.broadcasted_iota(jnp.int32, (d, LANES), 1)
    hi, lo = _split_bf16(jnp.where((r_idx % SB_HEADS == f_head) & (r_idx < nrow), out_t, 0.0))
    sel = jnp.where(lax.broadcasted_iota(jnp.int32, (16, LANES), 1) // SB_HEADS
                    == lax.broadcasted_iota(jnp.int32, (16, LANES), 0), 1.0, 0.0).astype(BF16)
    past = _dot_nt(sel, hi) + _dot_nt(sel, lo)
    o_ref[0] = jnp.concatenate(rows, axis=0) + past[:t_dec]


def _feature_major_pages(cache):
    n_pool, page, heads, hd = cache.shape
    return jnp.transpose(cache, (0, 2, 3, 1)).reshape(n_pool, heads * hd, page)


def _sb_sample(q, k_new, v_new, pages_kt, pages_vt, page_table, bias, t_dec, hd):
    n_s, d = q.shape
    n_seq = n_s // t_dec
    page = pages_kt.shape[2]
    n_pages = page_table.shape[1]
    nrow = t_dec * SB_HEADS
    assert page == LANES and nrow <= LANES and n_pages % 2 == 0 and t_dec <= 16
    bias_rows = jnp.broadcast_to(jnp.tile(bias.astype(F32), t_dec)[:, None], (nrow, LANES))
    bias_rows = jnp.concatenate([bias_rows, jnp.zeros((LANES - nrow, LANES), F32)], axis=0)
    seq3 = lambda a: a.reshape(n_seq, t_dec, d)
    seq_spec = pl.BlockSpec((1, t_dec, d), lambda b, pt: (b, 0, 0))

    def page_spec(j):
        return pl.BlockSpec((1, d, page), lambda b, pt: (pt[b * n_pages + j], 0, 0))

    kern = functools.partial(_sb_sample_kernel, t_dec=t_dec, hd=hd, n_pages=n_pages)
    out = pl.pallas_call(
        kern,
        grid_spec=pltpu.PrefetchScalarGridSpec(
            num_scalar_prefetch=1, grid=(n_seq,),
            in_specs=[seq_spec, seq_spec, seq_spec, pl.BlockSpec((LANES, LANES), lambda b, pt: (0, 0))]
                     + [page_spec(j) for j in range(n_pages)] + [page_spec(j) for j in range(n_pages)],
            out_specs=seq_spec),
        out_shape=jax.ShapeDtypeStruct((n_seq, t_dec, d), F32),
        compiler_params=_params("arbitrary"), name="sb_sample",
    )(page_table.reshape(-1), seq3(q), seq3(k_new), seq3(v_new), bias_rows,
      *([pages_kt] * n_pages), *([pages_vt] * n_pages))
    return out.reshape(n_s, d)


def kernel(x_prompt, x_sample, p_prompt, p_sample, state_ret, cache_k, cache_v, page_table, g_norm, w_ff1_gate, w_ff1_up, w_ff1_down, w_ff2_gate, w_ff2_up, w_ff2_down, w_ple_up, w_ple_gate, g_ple, w_ret_in, w_ret_out, g_ret_gn, g_kv, w_kv, w_sb_q, w_sb_out, b_sb, g_final):
    batch, t_prompt, d = x_prompt.shape
    n_dec, t_dec, _ = x_sample.shape
    depth = g_norm.shape[0]
    n_ret = w_ret_in.shape[0]
    dk = d // RET_HEADS
    dv = w_ret_out.shape[1] // RET_HEADS
    hd = d // SB_HEADS
    past_len = page_table.shape[1] * cache_k.shape[1]
    row = lambda a: a.reshape(1, -1)
    bf = lambda a: a.astype(BF16)

    xp = x_prompt.reshape(batch * t_prompt, d)
    xs = x_sample.reshape(n_dec * t_dec, d)
    x_out = [(d, F32)]
    states_p, states_s = [], None
    pages_kt, pages_vt = _feature_major_pages(cache_k), _feature_major_pages(cache_v)
    for i in range(depth):
        (xp, xs), = _tokenwise(_ffn_body, [(xp, xs)],
                               [row(g_norm[i, 0]), bf(w_ff1_gate[i]), bf(w_ff1_up[i]), bf(w_ff1_down[i])],
                               x_out, "ffn1")
        if i < n_ret:
            g_gn = row(g_ret_gn[i])
            xp, s_p = _retention_prompt(xp, batch, row(g_norm[i, 1]), bf(w_ret_in[i]), bf(w_ret_out[i]),
                                        g_gn, dk, dv)
            q, k, v, g = _rowwise(functools.partial(_ret_in_proj_body, qk=RET_HEADS * dk, vd=RET_HEADS * dv),
                                  [xs], [row(g_norm[i, 1]), bf(w_ret_in[i])],
                                  [(RET_HEADS * dk, F32), (RET_HEADS * dk, F32),
                                   (RET_HEADS * dv, F32), (RET_HEADS * dv, F32)], "ret_in_proj_sample")
            gated, states_s = _retention_sample(q, k, v, g, g_gn, state_ret, i, states_s, t_dec, past_len, dk, dv)
            xs, = _rowwise(_proj_residual_body, [xs, gated], [bf(w_ret_out[i])], x_out, "ret_out_proj_sample")
            states_p.append(s_p)
        else:
            l = i - n_ret
            (qp, qs), = _tokenwise(functools.partial(_sb_q_body, scale=LOG2E * hd ** -0.5), [(xp, xs)],
                                   [row(g_norm[i, 1]), bf(w_sb_q[l])], [(d, BF16)], "sb_q_proj")
            bias2 = b_sb[l].astype(F32) * LOG2E
            o_p = _sb_prompt(qp, kp16, vp16, bias2, batch, hd)
            o_s = _sb_sample(qs.astype(F32), ks32, vs32, pages_kt, pages_vt, page_table, bias2, t_dec, hd)
            (xp, xs), = _tokenwise(_proj_residual_body, [(xp, xs), (o_p, bf(o_s))], [bf(w_sb_out[l])],
                                   x_out, "sb_out_proj")
        p_p = p_prompt[i].reshape(batch * t_prompt, -1)
        p_s = p_sample[i].reshape(n_dec * t_dec, -1)
        consts = [row(g_norm[i, 2]), bf(w_ff2_gate[i]), bf(w_ff2_up[i]), bf(w_ff2_down[i]),
                  row(g_ple[i]), bf(w_ple_up[i]), bf(w_ple_gate[i])]
        if i == depth - 1:
            (xp, xs), = _tokenwise(functools.partial(_ffn_ple_body, final=True), [(xp, xs), (p_p, p_s)],
                                   consts + [row(g_final)], x_out, "ffn2_ple_final")
        else:
            (xp, xs), = _tokenwise(functools.partial(_ffn_ple_body, final=False), [(xp, xs), (p_p, p_s)],
                                   consts, x_out, "ffn2_ple")
        if i == n_ret - 1:
            kpt, vpt, kp16, vp16 = _kv_prompt(xp, batch, row(g_kv), bf(w_kv))
            ks32, vs32 = _rowwise(_kv_body, [xs], [row(g_kv), bf(w_kv)], [(d, F32), (d, F32)], "kv_proj_sample")
    yp, ys = xp, xs
    heads_last = lambda a: jnp.transpose(a.reshape(batch, SB_HEADS, hd, t_prompt), (0, 3, 1, 2))
    return (yp.reshape(batch, t_prompt, d), ys.reshape(n_dec, t_dec, d),
            jnp.stack(states_p), states_s,
            heads_last(kpt), heads_last(vpt),
            ks32.reshape(n_dec, t_dec, SB_HEADS, hd), vs32.reshape(n_dec, t_dec, SB_HEADS, hd))
```

```python
import functools
import math

import jax
import jax.numpy as jnp
from jax import lax
from jax.experimental import pallas as pl
from jax.experimental.pallas import tpu as pltpu

F32 = jnp.float32
BF16 = jnp.bfloat16

EPS = 1e-6
ROPE_BASE = 10000.0
RET_HEADS = 4
SB_HEADS = 16
LANES = 128
VMEM_LIMIT_BYTES = 58 * 1024 * 1024
MASKED_LOGIT = -1e30
LOG2E = 1.4426950408889634
TOKEN_TILE = 512
RET_CHUNK_ROWS = 512
SB_QUERY_BLOCK = 512
SB_KEY_BLOCK = 256
SB_HEADS_PER_STEP = 4
SB_BLOCKS_PER_VISIT = 2


def _params(*sem, flags=None):
    return pltpu.CompilerParams(dimension_semantics=sem, vmem_limit_bytes=VMEM_LIMIT_BYTES, flags=flags)


def _dot(a, b):
    return jnp.dot(a, b, preferred_element_type=F32)


def _dot_nt(a, b):
    return lax.dot_general(a, b, (((1,), (1,)), ((), ())), preferred_element_type=F32)


def _rmsnorm(x, g):
    y = x * lax.rsqrt(jnp.mean(x * x, axis=-1, keepdims=True) + EPS)
    return y * g


def _silu(x):
    return x * jax.nn.sigmoid(x)


def _softplus2(y):
    neg_abs = lax.bitcast_convert_type(lax.bitcast_convert_type(y, jnp.int32) | jnp.int32(-2 ** 31), F32)
    return jnp.maximum(y, 0.0) + jnp.log(1.0 + jnp.exp2(neg_abs)) * LOG2E


def _split_bf16(x):
    hi = x.astype(BF16)
    lo = (x - hi.astype(F32)).astype(BF16)
    return hi, lo


class _Layer:
    def __init__(self, stack, layer):
        self.stack, self.layer = stack, layer
        self.shape = stack.shape[1:]


def _array(a):
    return a.stack if isinstance(a, _Layer) else a


def _resident(arr):
    nd = len(arr.shape)
    if isinstance(arr, _Layer):
        return pl.BlockSpec((None,) + tuple(arr.shape), lambda *_: (arr.layer,) + (0,) * nd,
                            pipeline_mode=pl.Buffered(1))
    return pl.BlockSpec(arr.shape, lambda *_: (0,) * nd, pipeline_mode=pl.Buffered(1))


def _tokenwise(body, streams, consts, outs, name):
    n_p, n_s = streams[0][0].shape[0], streams[0][1].shape[0]
    tm = min(TOKEN_TILE, n_s)
    assert n_p % tm == 0 and n_s % tm == 0
    p_tiles, s_tiles = n_p // tm, n_s // tm
    ns, nc, no = len(streams), len(consts), len(outs)

    def kern(*refs):
        in_refs, c_refs, o_refs = refs[:2 * ns], refs[2 * ns:2 * ns + nc], refs[2 * ns + nc:]
        is_prompt = pl.program_id(0) < p_tiles
        vals = [jnp.where(is_prompt, in_refs[2 * k][...], in_refs[2 * k + 1][...]) for k in range(ns)]
        res = body(*vals, *c_refs)

        @pl.when(is_prompt)
        def _():
            for k in range(no):
                o_refs[2 * k][...] = res[k].astype(outs[k][1])

        @pl.when(jnp.logical_not(is_prompt))
        def _():
            for k in range(no):
                o_refs[2 * k + 1][...] = res[k].astype(outs[k][1])

    def p_spec(w, layer=None):
        if layer is not None:
            return pl.BlockSpec((None, tm, w), lambda i: (layer, jnp.minimum(i, p_tiles - 1), 0))
        return pl.BlockSpec((tm, w), lambda i: (jnp.minimum(i, p_tiles - 1), 0))

    def s_spec(w, layer=None):
        if layer is not None:
            return pl.BlockSpec((None, tm, w), lambda i: (layer, jnp.maximum(i - p_tiles, 0), 0))
        return pl.BlockSpec((tm, w), lambda i: (jnp.maximum(i - p_tiles, 0), 0))

    in_specs, args = [], []
    for a_p, a_s in streams:
        layer = a_p.layer if isinstance(a_p, _Layer) else None
        in_specs += [p_spec(a_p.shape[1], layer), s_spec(a_s.shape[1], layer)]
        args += [_array(a_p), _array(a_s)]
    in_specs += [_resident(c) for c in consts]
    consts = [_array(c) for c in consts]
    out_specs, out_shape = [], []
    for w, dt in outs:
        out_specs += [p_spec(w), s_spec(w)]
        out_shape += [jax.ShapeDtypeStruct((n_p, w), dt), jax.ShapeDtypeStruct((n_s, w), dt)]
    res = pl.pallas_call(
        kern, grid=(p_tiles + s_tiles,), in_specs=in_specs, out_specs=out_specs,
        out_shape=out_shape, compiler_params=_params("arbitrary"), name=name,
    )(*args, *consts)
    return [(res[2 * k], res[2 * k + 1]) for k in range(no)]


def _rowwise(body, rows, consts, outs, name):
    n = rows[0].shape[0]
    tm = min(TOKEN_TILE, n)
    assert n % tm == 0
    nr, nc, no = len(rows), len(consts), len(outs)

    def kern(*refs):
        res = body(*[r[...] for r in refs[:nr]], *refs[nr:nr + nc])
        for k in range(no):
            refs[nr + nc + k][...] = res[k].astype(outs[k][1])

    in_specs = [pl.BlockSpec((tm, a.shape[1]), lambda i: (i, 0)) for a in rows]
    in_specs += [_resident(c) for c in consts]
    consts = [_array(c) for c in consts]
    return pl.pallas_call(
        kern, grid=(n // tm,), in_specs=in_specs,
        out_specs=[pl.BlockSpec((tm, w), lambda i: (i, 0)) for w, _ in outs],
        out_shape=[jax.ShapeDtypeStruct((n, w), dt) for w, dt in outs],
        compiler_params=_params("arbitrary"), name=name,
    )(*rows, *consts)


def _ffn_body(x, g_ref, wg_ref, wu_ref, wd_ref):
    h = _rmsnorm(x, g_ref[...]).astype(BF16)
    d_ff = wg_ref.shape[1]
    chunk = 256 if d_ff % 256 == 0 else LANES
    acc = None
    for c0 in range(0, d_ff, chunk):
        gate = _dot(h, wg_ref[:, c0:c0 + chunk])
        up = _dot(h, wu_ref[:, c0:c0 + chunk])
        d = _dot((_silu(gate) * up).astype(BF16), wd_ref[c0:c0 + chunk, :])
        acc = d if acc is None else acc + d
    return (x + 0.5 * acc,)


def _ple_body(x, p, g_ref, wup_ref, wgate_ref):
    h = _rmsnorm(x, g_ref[...]).astype(BF16)
    gate = jax.nn.sigmoid(_dot(h, wgate_ref[...]))
    return (x + _dot(p.astype(BF16), wup_ref[...]) * gate,)


def _ffn_q_body(x, g_ref, wg_ref, wu_ref, wd_ref, gq_ref, wq_ref, *, scale):
    (x,) = _ffn_body(x, g_ref, wg_ref, wu_ref, wd_ref)
    return (x, _dot(_rmsnorm(x, gq_ref[...]).astype(BF16), wq_ref[...]) * scale)


def _kv_body(x, g_ref, w_ref):
    h = _rmsnorm(x, g_ref[...]).astype(BF16)
    half = w_ref.shape[1] // 2
    return (_dot(h, w_ref[:, :half]), _dot(h, w_ref[:, half:]))


def _kv_prompt_kernel(x_ref, g_ref, w_ref, kt_ref, vt_ref, k16_ref, v16_ref):
    k, v = _kv_body(x_ref[...], g_ref, w_ref)
    kt_ref[0] = k.T
    vt_ref[0] = v.T
    k16_ref[...] = k.astype(BF16)
    v16_ref[...] = v.astype(BF16)


def _kv_prompt(xp, batch, g, w):
    n_p, d = xp.shape
    t = n_p // batch
    tm = min(TOKEN_TILE, t)
    assert t % tm == 0
    nt = t // tm
    row_spec = pl.BlockSpec((tm, d), lambda b, i: (b * nt + i, 0))
    fm_spec = pl.BlockSpec((1, d, tm), lambda b, i: (b, 0, i))
    return pl.pallas_call(
        _kv_prompt_kernel, grid=(batch, nt),
        in_specs=[row_spec, _resident(g), _resident(w)],
        out_specs=[fm_spec, fm_spec, row_spec, row_spec],
        out_shape=[jax.ShapeDtypeStruct((batch, d, t), F32), jax.ShapeDtypeStruct((batch, d, t), F32),
                   jax.ShapeDtypeStruct((n_p, d), BF16), jax.ShapeDtypeStruct((n_p, d), BF16)],
        compiler_params=_params("arbitrary", "arbitrary"), name="kv_proj_prompt",
    )(xp, g, w)


def _proj_residual_body(x, a, w_ref):
    return (x + _dot(a.astype(BF16), w_ref[...]),)


def _ffn_ple_body(x, p, *rest, mixer_proj, final):
    rest = list(rest)
    mixed = rest.pop(0) if mixer_proj else None
    ffn_refs, ple_refs, tail = rest[:4], rest[4:7], rest[7:]
    if mixer_proj:
        x = x + _dot(mixed, tail.pop(0)[...])
    (x,) = _ffn_body(x, *ffn_refs)
    (x,) = _ple_body(x, p, *ple_refs)
    return (_rmsnorm(x, tail[0][...]) if final else x,)


def _ret_in_proj_body(x, g_ref, w_ref, *, qk, vd):
    h = _rmsnorm(x, g_ref[...]).astype(BF16)
    return (_dot(h, w_ref[:, :qk]), _dot(h, w_ref[:, qk:2 * qk]),
            _dot(h, w_ref[:, 2 * qk:2 * qk + vd]), _dot(h, w_ref[:, 2 * qk + vd:]))


def _rope_tables(pos, half):
    inv_freq = ROPE_BASE ** (-jnp.arange(half, dtype=F32) / half)
    ang = pos.astype(F32)[:, None] * inv_freq[None, :]
    return jnp.cos(ang), jnp.sin(ang)


def _rope(x, cos, sin):
    half = x.shape[-1] // 2
    x1, x2 = x[:, :half], x[:, half:]
    return jnp.concatenate([x1 * cos - x2 * sin, x2 * cos + x1 * sin], axis=-1)


def _log_decay(head):
    return math.log(1.0 - 2.0 ** (-5.0 - head))


def _group_norm_gate(o, g, ggn):
    mu = jnp.mean(o, axis=-1, keepdims=True)
    var = jnp.mean(jnp.square(o - mu), axis=-1, keepdims=True)
    on = (o - mu) * lax.rsqrt(var + EPS) * ggn
    return _silu(g) * on


def _ret_prompt_kernel(x_ref, gn_ref, win_ref, wout_ref, ggn_ref, cos_ref, sin_ref,
                       o_ref, s_ref, *, rows, dk, dv):
    qk, vd = RET_HEADS * dk, RET_HEADS * dv

    @pl.when(pl.program_id(1) == 0)
    def _():
        s_ref[...] = jnp.zeros_like(s_ref)

    x = x_ref[...]
    h = _rmsnorm(x, gn_ref[...]).astype(BF16)
    cos, sin = cos_ref[...], sin_ref[...]
    diff = (lax.broadcasted_iota(jnp.int32, (rows, rows), 0)
            - lax.broadcasted_iota(jnp.int32, (rows, rows), 1)).astype(F32)
    ri = lax.broadcasted_iota(jnp.int32, (rows, 1), 0).astype(F32)
    acc = None
    for hh in range(RET_HEADS):
        lg = _log_decay(hh)
        q = _rope(_dot(h, win_ref[:, hh * dk:(hh + 1) * dk]), cos, sin)
        k = _rope(_dot(h, win_ref[:, qk + hh * dk:qk + (hh + 1) * dk]), cos, sin) * (dk ** -0.5)
        v = _dot(h, win_ref[:, 2 * qk + hh * dv:2 * qk + (hh + 1) * dv]).astype(BF16)
        g = _dot(h, win_ref[:, 2 * qk + vd + hh * dv:2 * qk + vd + (hh + 1) * dv])
        decay = jnp.where(diff >= 0, jnp.exp(jnp.maximum(diff, 0.0) * lg), 0.0)
        s = _dot_nt(q.astype(BF16), k.astype(BF16)) * decay
        state = s_ref[0, hh]
        q_dec = (q * jnp.exp((ri + 1.0) * lg)).astype(BF16)
        o = _dot(s.astype(BF16), v) + _dot(q_dec, state.astype(BF16))
        k_dec = k * jnp.exp((rows - 1.0 - ri) * lg)
        s_ref[0, hh] = math.exp(rows * lg) * state + _dot(k_dec.T.astype(BF16), v)
        gated = _group_norm_gate(o, g, ggn_ref[:, hh * dv:(hh + 1) * dv]).astype(BF16)
        d = _dot(gated, wout_ref[hh * dv:(hh + 1) * dv, :])
        acc = d if acc is None else acc + d
    o_ref[...] = x + acc


def _retention_prompt(xp, batch, g, w_in, w_out, g_gn, dk, dv):
    n_p, d = xp.shape
    t = n_p // batch
    rows = min(RET_CHUNK_ROWS, t)
    assert t % rows == 0
    nch = t // rows
    cos, sin = _rope_tables(jnp.arange(t, dtype=jnp.int32), dk // 2)
    kern = functools.partial(_ret_prompt_kernel, rows=rows, dk=dk, dv=dv)
    return pl.pallas_call(
        kern, grid=(batch, nch),
        in_specs=[pl.BlockSpec((rows, d), lambda b, c: (b * nch + c, 0)),
                  _resident(g), _resident(w_in), _resident(w_out), _resident(g_gn),
                  pl.BlockSpec((rows, dk // 2), lambda b, c: (c, 0)),
                  pl.BlockSpec((rows, dk // 2), lambda b, c: (c, 0))],
        out_specs=[pl.BlockSpec((rows, d), lambda b, c: (b * nch + c, 0)),
                   pl.BlockSpec((1, RET_HEADS, dk, dv), lambda b, c: (b, 0, 0, 0))],
        out_shape=[jax.ShapeDtypeStruct((n_p, d), F32),
                   jax.ShapeDtypeStruct((batch, RET_HEADS, dk, dv), F32)],
        compiler_params=_params("arbitrary", "arbitrary"), name="retention_prompt",
    )(xp, g, _array(w_in), _array(w_out), g_gn, cos, sin)


def _pad_rows(x, rows):
    return jnp.concatenate([x, jnp.zeros((rows - x.shape[0], x.shape[1]), x.dtype)], axis=0)


def _ret_sample_kernel(q_ref, k_ref, v_ref, g_ref, ggn_ref, cos_ref, sin_ref, s_ref, *rest,
                       t_dec, dk, dv, n_fill):
    o_ref, so_ref = rest[-2], rest[-1]
    nrow = 2 * t_dec
    cos, sin = cos_ref[...], sin_ref[...]
    r_i = lax.broadcasted_iota(jnp.int32, (nrow, LANES), 0)
    c_i = lax.broadcasted_iota(jnp.int32, (nrow, LANES), 1)
    same_seq = (r_i // t_dec) == (c_i // t_dec)
    diff = ((r_i % t_dec) - (c_i % t_dec)).astype(F32)
    valid = same_seq & (diff >= 0) & (c_i < nrow)
    t_row = (lax.broadcasted_iota(jnp.int32, (nrow, 1), 0) % t_dec).astype(F32)
    seq_row = lax.broadcasted_iota(jnp.int32, (nrow, 1), 0) // t_dec
    seq_lane = lax.broadcasted_iota(jnp.int32, (1, LANES), 1) // t_dec
    for fill in range(n_fill):
        so_ref[1 + fill] = jnp.zeros(so_ref.shape[1:], so_ref.dtype)
    outs = []
    for hh in range(RET_HEADS):
        lg = _log_decay(hh)
        q = _rope(q_ref[0, :, hh * dk:(hh + 1) * dk], cos, sin)
        k = _rope(k_ref[0, :, hh * dk:(hh + 1) * dk], cos, sin) * (dk ** -0.5)
        v = _pad_rows(v_ref[0, :, hh * dv:(hh + 1) * dv], LANES).astype(BF16)
        g = g_ref[0, :, hh * dv:(hh + 1) * dv]
        decay = jnp.where(valid, jnp.exp(jnp.maximum(diff, 0.0) * lg), 0.0)
        q16 = _pad_rows(q, 16)
        s = _dot_nt(q16.astype(BF16), _pad_rows(k, LANES).astype(BF16))[:nrow] * decay
        o = _dot(_pad_rows(s, 16).astype(BF16), v)[:nrow]
        q_dec = _pad_rows(q * jnp.exp((t_row + 1.0) * lg), 16).astype(BF16)
        k_dec_t = _pad_rows(k * jnp.exp((t_dec - 1.0 - t_row) * lg), LANES).T
        for sq in range(2):
            state = s_ref[0, sq, hh]
            o_state = _dot(q_dec, state.astype(BF16))[:nrow]
            o = o + jnp.where(seq_row == sq, o_state, 0.0)
            k_sq = jnp.where(seq_lane == sq, k_dec_t, 0.0).astype(BF16)
            so_ref[0, sq, hh] = math.exp(t_dec * lg) * state + _dot(k_sq, v)
        outs.append(_group_norm_gate(o, g, ggn_ref[:, hh * dv:(hh + 1) * dv]))
    o_ref[0] = jnp.concatenate(outs, axis=-1)


def _retention_sample(q, k, v, g, g_gn, state_all, layer, prev_states, t_dec, past_len, dk, dv):
    n_s = q.shape[0]
    n_seq = n_s // t_dec
    n_layers = state_all.shape[0]
    assert n_seq % 2 == 0
    nrow = 2 * t_dec
    pos = past_len + (jnp.arange(nrow, dtype=jnp.int32) % t_dec)
    cos, sin = _rope_tables(pos, dk // 2)
    grp = lambda a: a.reshape(n_seq // 2, nrow, a.shape[1])
    qk, vd = RET_HEADS * dk, RET_HEADS * dv
    first = prev_states is None
    n_fill = n_layers - 1 if first else 0
    kern = functools.partial(_ret_sample_kernel, t_dec=t_dec, dk=dk, dv=dv, n_fill=n_fill)
    row_spec = lambda w: pl.BlockSpec((1, nrow, w), lambda i: (i, 0, 0))
    st_in = pl.BlockSpec((1, 2, RET_HEADS, dk, dv), lambda i: (layer, i, 0, 0, 0))
    st_out = pl.BlockSpec((1 + n_fill, 2, RET_HEADS, dk, dv), lambda i: (0 if first else layer, i, 0, 0, 0))
    in_specs = [row_spec(qk), row_spec(qk), row_spec(vd), row_spec(vd), _resident(g_gn),
                _resident(cos), _resident(sin), st_in]
    args = [grp(q), grp(k), grp(v), grp(g), g_gn, cos, sin, state_all]
    aliases = {}
    if not first:
        in_specs.append(pl.BlockSpec(memory_space=pl.ANY))
        args.append(prev_states)
        aliases = {len(args) - 1: 1}
    gated, s_new = pl.pallas_call(
        kern, grid=(n_seq // 2,), in_specs=in_specs,
        out_specs=[row_spec(vd), st_out],
        out_shape=[jax.ShapeDtypeStruct((n_seq // 2, nrow, vd), F32),
                   jax.ShapeDtypeStruct(state_all.shape, F32)],
        input_output_aliases=aliases,
        compiler_params=_params("arbitrary"), name="retention_sample",
    )(*args)
    return gated.reshape(n_s, vd), s_new


def _suffix_ones(n):
    r = lax.broadcasted_iota(jnp.int32, (n, n), 0)
    c = lax.broadcasted_iota(jnp.int32, (n, n), 1)
    return jnp.where(r >= c, 1.0, 0.0).astype(BF16)


def _sb_blocks(ys, ones):
    parts = [_split_bf16(_softplus2(y)) for y in ys]
    suffixes = [_dot(hi, ones) + _dot(lo, ones) for hi, lo in parts]
    return [(jnp.exp2(y - sfx), sfx[:, :1]) for y, sfx in zip(ys, suffixes)]


def _sb_blocks_skewed(n, logits, ones, use):
    def suffix(y):
        hi, lo = _split_bf16(_softplus2(y))
        return _dot(hi, ones) + _dot(lo, ones)

    ys, sfx, results = {}, {}, []
    for step in range(n + 2):
        if step < n:
            ys[step] = logits(step)
        if 1 <= step <= n:
            sfx[step - 1] = suffix(ys[step - 1])
        if step >= 2:
            j = step - 2
            results.append(use(j, jnp.exp2(ys[j] - sfx[j]), sfx[j][:, :1]))
    return results


def _sb_prompt_kernel(bias_ref, q_ref, k_ref, v_ref, o_ref, qh_ref, ones_ref, bm_ref, acc_ref, carry_ref,
                      *, tq, tk, hd, n_pairs):
    grp, qi = pl.program_id(1), pl.program_id(2)
    ratio = tq // tk
    lo_lanes = lax.broadcasted_iota(jnp.int32, (1, LANES), 1) < hd
    row = lax.broadcasted_iota(jnp.int32, (tq, tk), 0)
    col = lax.broadcasted_iota(jnp.int32, (tq, tk), 1)
    for p in range(n_pairs):
        qp = q_ref[:, p * LANES:(p + 1) * LANES]
        zero = jnp.zeros_like(qp)
        qh_ref[2 * p] = jnp.where(lo_lanes, qp, zero)
        qh_ref[2 * p + 1] = jnp.where(lo_lanes, zero, qp)

    @pl.when(qi == 0)
    def _():
        for h in range(2 * n_pairs):
            bias = bias_ref[grp * 2 * n_pairs + h]
            bm_ref[h, 0] = jnp.full((tq, tk), bias, F32)
            for d in range(ratio):
                bm_ref[h, 1 + d] = jnp.where(col + d * tk < row, bias, MASKED_LOGIT)
        ones_ref[...] = _suffix_ones(tk)

    acc_ref[...] = jnp.zeros_like(acc_ref)
    carry_ref[...] = jnp.zeros_like(carry_ref)
    n_blocks = ratio * qi + ratio
    n_heads = 2 * n_pairs

    def visit(top, count):
        ones = ones_ref[...]
        starts = [pl.multiple_of((top - u) * tk, tk) for u in range(count)]
        kinds = [jnp.clip(top - u - ratio * qi + 1, 0, ratio) for u in range(count)]

        def logits(i):
            u, h = divmod(i, n_heads)
            kb = k_ref[pl.ds(starts[u], tk), (h // 2) * LANES:(h // 2 + 1) * LANES]
            return _dot_nt(qh_ref[h], kb) + bm_ref[h, kinds[u]]

        def weigh(i, a, tot):
            u, h = divmod(i, n_heads)
            vb = v_ref[pl.ds(starts[u], tk), (h // 2) * LANES:(h // 2 + 1) * LANES]
            vzero = jnp.zeros_like(vb)
            vh = jnp.where(lo_lanes, vb, vzero) if h % 2 == 0 else jnp.where(lo_lanes, vzero, vb)
            return _dot(a.astype(BF16), vh), tot

        blocks = _sb_blocks_skewed(count * n_heads, logits, ones, weigh)
        for p in range(n_pairs):
            c = carry_ref[p]
            acc = acc_ref[:, p * LANES:(p + 1) * LANES]
            for u in range(count):
                (d0, t0), (d1, t1) = blocks[u * n_heads + 2 * p], blocks[u * n_heads + 2 * p + 1]
                acc = acc + jnp.exp2(-c) * (d0 + d1)
                c = c + jnp.where(lo_lanes, t0, t1)
            acc_ref[:, p * LANES:(p + 1) * LANES] = acc
            carry_ref[p] = c

    big = SB_BLOCKS_PER_VISIT
    rem = lax.rem(n_blocks, big)
    if big > ratio:
        @pl.when(rem != 0)
        def _():
            visit(n_blocks - 1, ratio)

    def body(it, _):
        visit(n_blocks - 1 - rem - it * big, big)
        return 0

    lax.fori_loop(0, n_blocks // big, body, 0)
    o_ref[...] = acc_ref[...].astype(o_ref.dtype)


def _sb_prompt(q, k, v, bias, batch, hd):
    n_p, d = q.shape
    t = n_p // batch
    tq, tk = min(SB_QUERY_BLOCK, t), min(SB_KEY_BLOCK, t)
    n_pairs = SB_HEADS_PER_STEP // 2
    width = n_pairs * LANES
    assert t % tq == 0 and tq % tk == 0 and 2 * hd == LANES and d % width == 0
    assert SB_BLOCKS_PER_VISIT % (tq // tk) == 0 and SB_BLOCKS_PER_VISIT <= 2 * (tq // tk)
    nq = t // tq
    kern = functools.partial(_sb_prompt_kernel, tq=tq, tk=tk, hd=hd, n_pairs=n_pairs)
    return pl.pallas_call(
        kern, grid=(batch, d // width, nq),
        in_specs=[pl.BlockSpec(memory_space=pltpu.SMEM),
                  pl.BlockSpec((tq, width), lambda b, g, i: (b * nq + i, g)),
                  pl.BlockSpec((t, width), lambda b, g, i: (b, g)),
                  pl.BlockSpec((t, width), lambda b, g, i: (b, g))],
        out_specs=pl.BlockSpec((tq, width), lambda b, g, i: (b * nq + i, g)),
        out_shape=jax.ShapeDtypeStruct((n_p, d), BF16),
        scratch_shapes=[pltpu.VMEM((2 * n_pairs, tq, LANES), BF16), pltpu.VMEM((tk, tk), BF16),
                        pltpu.VMEM((2 * n_pairs, 1 + tq // tk, tq, tk), F32), pltpu.VMEM((tq, width), F32),
                        pltpu.VMEM((n_pairs, tq, LANES), F32)],
        compiler_params=_params("arbitrary", "arbitrary", "arbitrary"), name="sb_prompt",
    )(bias, q, k, v)


def _sb_sample_kernel(pt_ref, q_ref, kn_ref, vn_ref, bias_ref, *refs, t_dec, hd, n_pages):
    del pt_ref
    kt_refs, vt_refs, o_ref = refs[:n_pages], refs[n_pages:2 * n_pages], refs[2 * n_pages]
    page = kt_refs[0].shape[2]
    nrow = t_dec * SB_HEADS
    d = SB_HEADS * hd
    blk = 2 * page
    head_mask = ((lax.broadcasted_iota(jnp.int32, (SB_HEADS, d), 1) // hd)
                 == lax.broadcasted_iota(jnp.int32, (SB_HEADS, d), 0))
    bias = bias_ref[:, :1]
    qv = q_ref[0]
    qbd = jnp.concatenate([jnp.where(head_mask, qv[t:t + 1, :], 0.0) for t in range(t_dec)]
                          + [jnp.zeros((LANES - nrow, d), F32)], axis=0).astype(BF16)
    key_i = lax.broadcasted_iota(jnp.int32, (LANES, blk), 1)
    qry_t = lax.broadcasted_iota(jnp.int32, (LANES, blk), 0) // SB_HEADS
    y_new = _dot_nt(qbd, _pad_rows(kn_ref[0], blk).astype(BF16)) + bias
    y_pages = [_dot(qbd, kt[0].astype(BF16)) for kt in kt_refs]
    ys = [jnp.where(key_i < qry_t, y_new, MASKED_LOGIT)]
    ys += [jnp.concatenate([y_pages[2 * i], y_pages[2 * i + 1]], axis=1) + bias for i in reversed(range(n_pages // 2))]
    blocks = _sb_blocks(ys, _suffix_ones(blk))
    carry = jnp.zeros((LANES, 1), F32)
    weights = []
    for a, tot in blocks:
        weights.append((a * jnp.exp2(-carry)).astype(BF16))
        carry = carry + tot
    out_new = _dot(weights[0], _pad_rows(vn_ref[0], blk).astype(BF16))
    out_t = None
    for w, i in zip(weights[1:], reversed(range(n_pages // 2))):
        v_t = jnp.concatenate([vt_refs[2 * i][0], vt_refs[2 * i + 1][0]], axis=1).astype(BF16)
        part = _dot_nt(v_t, w)
        out_t = part if out_t is None else out_t + part
    rows = [jnp.sum(jnp.where(head_mask, out_new[t * SB_HEADS:(t + 1) * SB_HEADS, :], 0.0), axis=0, keepdims=True)
            for t in range(t_dec)]
    f_head = lax.broadcasted_iota(jnp.int32, (d, LANES), 0) // hd
    r_idx = lax.broadcasted_iota(jnp.int32, (d, LANES), 1)
    hi, lo = _split_bf16(jnp.where((r_idx % SB_HEADS == f_head) & (r_idx < nrow), out_t, 0.0))
    sel = jnp.where(lax.broadcasted_iota(jnp.int32, (16, LANES), 1) // SB_HEADS
                    == lax.broadcasted_iota(jnp.int32, (16, LANES), 0), 1.0, 0.0).astype(BF16)
    past = _dot_nt(sel, hi) + _dot_nt(sel, lo)
    o_ref[0] = jnp.concatenate(rows, axis=0) + past[:t_dec]


def _feature_major_pages(cache):
    n_pool, page, heads, hd = cache.shape
    return jnp.transpose(cache, (0, 2, 3, 1)).reshape(n_pool, heads * hd, page)


def _sb_sample(q, k_new, v_new, pages_kt, pages_vt, page_table, bias, t_dec, hd):
    n_s, d = q.shape
    n_seq = n_s // t_dec
    page = pages_kt.shape[2]
    n_pages = page_table.shape[1]
    nrow = t_dec * SB_HEADS
    assert page == LANES and nrow <= LANES and n_pages % 2 == 0 and t_dec <= 16
    bias_rows = jnp.broadcast_to(jnp.tile(bias.astype(F32), t_dec)[:, None], (nrow, LANES))
    bias_rows = jnp.concatenate([bias_rows, jnp.zeros((LANES - nrow, LANES), F32)], axis=0)
    seq3 = lambda a: a.reshape(n_seq, t_dec, d)
    seq_spec = pl.BlockSpec((1, t_dec, d), lambda b, pt: (b, 0, 0))

    def page_spec(j):
        return pl.BlockSpec((1, d, page), lambda b, pt: (pt[b * n_pages + j], 0, 0))

    kern = functools.partial(_sb_sample_kernel, t_dec=t_dec, hd=hd, n_pages=n_pages)
    out = pl.pallas_call(
        kern,
        grid_spec=pltpu.PrefetchScalarGridSpec(
            num_scalar_prefetch=1, grid=(n_seq,),
            in_specs=[seq_spec, seq_spec, seq_spec, pl.BlockSpec((LANES, LANES), lambda b, pt: (0, 0))]
                     + [page_spec(j) for j in range(n_pages)] + [page_spec(j) for j in range(n_pages)],
            out_specs=seq_spec),
        out_shape=jax.ShapeDtypeStruct((n_seq, t_dec, d), F32),
        compiler_params=_params("arbitrary"), name="sb_sample",
    )(page_table.reshape(-1), seq3(q), seq3(k_new), seq3(v_new), bias_rows,
      *([pages_kt] * n_pages), *([pages_vt] * n_pages))
    return out.reshape(n_s, d)


def kernel(x_prompt, x_sample, p_prompt, p_sample, state_ret, cache_k, cache_v, page_table, g_norm, w_ff1_gate, w_ff1_up, w_ff1_down, w_ff2_gate, w_ff2_up, w_ff2_down, w_ple_up, w_ple_gate, g_ple, w_ret_in, w_ret_out, g_ret_gn, g_kv, w_kv, w_sb_q, w_sb_out, b_sb, g_final):
    batch, t_prompt, d = x_prompt.shape
    n_dec, t_dec, _ = x_sample.shape
    depth = g_norm.shape[0]
    n_ret = w_ret_in.shape[0]
    dk = d // RET_HEADS
    dv = w_ret_out.shape[1] // RET_HEADS
    hd = d // SB_HEADS
    past_len = page_table.shape[1] * cache_k.shape[1]
    row = lambda a: a.reshape(1, -1)
    bf = lambda a: a.astype(BF16)

    xp = x_prompt.reshape(batch * t_prompt, d)
    xs = x_sample.reshape(n_dec * t_dec, d)
    x_out = [(d, F32)]
    states_p, states_s = [], None
    pages_kt, pages_vt = _feature_major_pages(cache_k), _feature_major_pages(cache_v)
    ff1 = [bf(w_ff1_gate), bf(w_ff1_up), bf(w_ff1_down)]
    ff2 = [bf(w_ff2_gate), bf(w_ff2_up), bf(w_ff2_down)]
    ple = [bf(w_ple_up), bf(w_ple_gate)]
    ret_in, ret_out, sb_q, sb_out = bf(w_ret_in), bf(w_ret_out), bf(w_sb_q), bf(w_sb_out)
    p_p = p_prompt.reshape(depth, batch * t_prompt, -1)
    p_s = p_sample.reshape(depth, n_dec * t_dec, -1)
    for i in range(depth):
        ffn1_consts = [row(g_norm[i, 0])] + [_Layer(w, i) for w in ff1]
        streams = [(xp, xs), (_Layer(p_p, i), _Layer(p_s, i))]
        if i < n_ret:
            (xp, xs), = _tokenwise(_ffn_body, [(xp, xs)], ffn1_consts, x_out, "ffn1")
            g_gn = row(g_ret_gn[i])
            xp, s_p = _retention_prompt(xp, batch, row(g_norm[i, 1]), _Layer(ret_in, i), _Layer(ret_out, i),
                                        g_gn, dk, dv)
            q, k, v, g = _rowwise(functools.partial(_ret_in_proj_body, qk=RET_HEADS * dk, vd=RET_HEADS * dv),
                                  [xs], [row(g_norm[i, 1]), _Layer(ret_in, i)],
                                  [(RET_HEADS * dk, F32), (RET_HEADS * dk, F32),
                                   (RET_HEADS * dv, F32), (RET_HEADS * dv, F32)], "ret_in_proj_sample")
            gated, states_s = _retention_sample(q, k, v, g, g_gn, state_ret, i, states_s, t_dec, past_len, dk, dv)
            xs, = _rowwise(_proj_residual_body, [xs, gated], [_Layer(ret_out, i)], x_out, "ret_out_proj_sample")
            states_p.append(s_p)
            streams[0] = (xp, xs)
            mixer_out = []
        else:
            l = i - n_ret
            (xp, xs), (qp, qs) = _tokenwise(functools.partial(_ffn_q_body, scale=LOG2E * hd ** -0.5), [(xp, xs)],
                                            ffn1_consts + [row(g_norm[i, 1]), _Layer(sb_q, l)],
                                            x_out + [(d, BF16)], "ffn1_q")
            bias2 = b_sb[l].astype(F32) * LOG2E
            o_p = _sb_prompt(qp, kp16, vp16, bias2, batch, hd)
            o_s = _sb_sample(qs.astype(F32), ks32, vs32, pages_kt, pages_vt, page_table, bias2, t_dec, hd)
            streams = [(xp, xs), streams[1], (o_p, bf(o_s))]
            mixer_out = [_Layer(sb_out, l)]
        last = i == depth - 1
        consts = ([row(g_norm[i, 2])] + [_Layer(w, i) for w in ff2] + [row(g_ple[i])] + [_Layer(w, i) for w in ple]
                  + mixer_out + ([row(g_final)] if last else []))
        (xp, xs), = _tokenwise(functools.partial(_ffn_ple_body, mixer_proj=bool(mixer_out), final=last),
                               streams, consts, x_out, "ffn2_ple")
        if i == n_ret - 1:
            kpt, vpt, kp16, vp16 = _kv_prompt(xp, batch, row(g_kv), bf(w_kv))
            ks32, vs32 = _rowwise(_kv_body, [xs], [row(g_kv), bf(w_kv)], [(d, F32), (d, F32)], "kv_proj_sample")
    yp, ys = xp, xs
    heads_last = lambda a: jnp.transpose(a.reshape(batch, SB_HEADS, hd, t_prompt), (0, 3, 1, 2))
    return (yp.reshape(batch, t_prompt, d), ys.reshape(n_dec, t_dec, d),
            jnp.stack(states_p), states_s,
            heads_last(kpt), heads_last(vpt),
            ks32.reshape(n_dec, t_dec, SB_HEADS, hd), vs32.reshape(n_dec, t_dec, SB_HEADS, hd))
```

```python
import functools
import math

import jax
import jax.numpy as jnp
from jax import lax
from jax.experimental import pallas as pl
from jax.experimental.pallas import tpu as pltpu

F32 = jnp.float32
BF16 = jnp.bfloat16

EPS = 1e-6
ROPE_BASE = 10000.0
RET_HEADS = 4
SB_HEADS = 16
LANES = 128
VMEM_LIMIT_BYTES = 58 * 1024 * 1024
MASKED_LOGIT = -1e30
LOG2E = 1.4426950408889634
TOKEN_TILE = 512
RET_CHUNK_ROWS = 512
SB_QUERY_BLOCK = 512
SB_KEY_BLOCK = 256
SB_HEADS_PER_STEP = 4
SB_BLOCKS_PER_VISIT = 2


def _params(*sem, flags=None):
    return pltpu.CompilerParams(dimension_semantics=sem, vmem_limit_bytes=VMEM_LIMIT_BYTES, flags=flags)


def _dot(a, b):
    return jnp.dot(a, b, preferred_element_type=F32)


def _dot_nt(a, b):
    return lax.dot_general(a, b, (((1,), (1,)), ((), ())), preferred_element_type=F32)


def _rmsnorm(x, g):
    y = x * lax.rsqrt(jnp.mean(x * x, axis=-1, keepdims=True) + EPS)
    return y * g


def _silu(x):
    return x * jax.nn.sigmoid(x)


def _softplus2(y):
    neg_abs = lax.bitcast_convert_type(lax.bitcast_convert_type(y, jnp.int32) | jnp.int32(-2 ** 31), F32)
    return jnp.maximum(y, 0.0) + jnp.log(1.0 + jnp.exp2(neg_abs)) * LOG2E


def _split_bf16(x):
    hi = x.astype(BF16)
    lo = (x - hi.astype(F32)).astype(BF16)
    return hi, lo


class _Layer:
    def __init__(self, stack, layer):
        self.stack, self.layer = stack, layer
        self.shape = stack.shape[1:]


def _array(a):
    return a.stack if isinstance(a, _Layer) else a


def _resident(arr):
    nd = len(arr.shape)
    if isinstance(arr, _Layer):
        return pl.BlockSpec((None,) + tuple(arr.shape), lambda *_: (arr.layer,) + (0,) * nd,
                            pipeline_mode=pl.Buffered(1))
    return pl.BlockSpec(arr.shape, lambda *_: (0,) * nd, pipeline_mode=pl.Buffered(1))


def _tokenwise(body, streams, consts, outs, name):
    n_p, n_s = streams[0][0].shape[0], streams[0][1].shape[0]
    tm = min(TOKEN_TILE, n_s)
    assert n_p % tm == 0 and n_s % tm == 0
    p_tiles, s_tiles = n_p // tm, n_s // tm
    ns, nc, no = len(streams), len(consts), len(outs)

    def kern(*refs):
        in_refs, c_refs, o_refs = refs[:2 * ns], refs[2 * ns:2 * ns + nc], refs[2 * ns + nc:]
        is_prompt = pl.program_id(0) < p_tiles
        vals = [jnp.where(is_prompt, in_refs[2 * k][...], in_refs[2 * k + 1][...]) for k in range(ns)]
        res = body(*vals, *c_refs)

        @pl.when(is_prompt)
        def _():
            for k in range(no):
                o_refs[2 * k][...] = res[k].astype(outs[k][1])

        @pl.when(jnp.logical_not(is_prompt))
        def _():
            for k in range(no):
                o_refs[2 * k + 1][...] = res[k].astype(outs[k][1])

    def p_spec(w, layer=None):
        if layer is not None:
            return pl.BlockSpec((None, tm, w), lambda i: (layer, jnp.minimum(i, p_tiles - 1), 0))
        return pl.BlockSpec((tm, w), lambda i: (jnp.minimum(i, p_tiles - 1), 0))

    def s_spec(w, layer=None):
        if layer is not None:
            return pl.BlockSpec((None, tm, w), lambda i: (layer, jnp.maximum(i - p_tiles, 0), 0))
        return pl.BlockSpec((tm, w), lambda i: (jnp.maximum(i - p_tiles, 0), 0))

    in_specs, args = [], []
    for a_p, a_s in streams:
        layer = a_p.layer if isinstance(a_p, _Layer) else None
        in_specs += [p_spec(a_p.shape[1], layer), s_spec(a_s.shape[1], layer)]
        args += [_array(a_p), _array(a_s)]
    in_specs += [_resident(c) for c in consts]
    consts = [_array(c) for c in consts]
    out_specs, out_shape = [], []
    for w, dt in outs:
        out_specs += [p_spec(w), s_spec(w)]
        out_shape += [jax.ShapeDtypeStruct((n_p, w), dt), jax.ShapeDtypeStruct((n_s, w), dt)]
    res = pl.pallas_call(
        kern, grid=(p_tiles + s_tiles,), in_specs=in_specs, out_specs=out_specs,
        out_shape=out_shape, compiler_params=_params("arbitrary"), name=name,
    )(*args, *consts)
    return [(res[2 * k], res[2 * k + 1]) for k in range(no)]


def _rowwise(body, rows, consts, outs, name):
    n = rows[0].shape[0]
    tm = min(TOKEN_TILE, n)
    assert n % tm == 0
    nr, nc, no = len(rows), len(consts), len(outs)

    def kern(*refs):
        res = body(*[r[...] for r in refs[:nr]], *refs[nr:nr + nc])
        for k in range(no):
            refs[nr + nc + k][...] = res[k].astype(outs[k][1])

    in_specs = [pl.BlockSpec((tm, a.shape[1]), lambda i: (i, 0)) for a in rows]
    in_specs += [_resident(c) for c in consts]
    consts = [_array(c) for c in consts]
    return pl.pallas_call(
        kern, grid=(n // tm,), in_specs=in_specs,
        out_specs=[pl.BlockSpec((tm, w), lambda i: (i, 0)) for w, _ in outs],
        out_shape=[jax.ShapeDtypeStruct((n, w), dt) for w, dt in outs],
        compiler_params=_params("arbitrary"), name=name,
    )(*rows, *consts)


def _ffn_body(x, g_ref, wg_ref, wu_ref, wd_ref):
    h = _rmsnorm(x, g_ref[...]).astype(BF16)
    d_ff = wg_ref.shape[1]
    chunk = 256 if d_ff % 256 == 0 else LANES
    acc = None
    for c0 in range(0, d_ff, chunk):
        gate = _dot(h, wg_ref[:, c0:c0 + chunk])
        up = _dot(h, wu_ref[:, c0:c0 + chunk])
        d = _dot((_silu(gate) * up).astype(BF16), wd_ref[c0:c0 + chunk, :])
        acc = d if acc is None else acc + d
    return (x + 0.5 * acc,)


def _ple_body(x, p, g_ref, wup_ref, wgate_ref):
    h = _rmsnorm(x, g_ref[...]).astype(BF16)
    gate = jax.nn.sigmoid(_dot(h, wgate_ref[...]))
    return (x + _dot(p.astype(BF16), wup_ref[...]) * gate,)


def _ffn_q_body(x, g_ref, wg_ref, wu_ref, wd_ref, gq_ref, wq_ref, *, scale):
    (x,) = _ffn_body(x, g_ref, wg_ref, wu_ref, wd_ref)
    return (x, _dot(_rmsnorm(x, gq_ref[...]).astype(BF16), wq_ref[...]) * scale)


def _kv_body(x, g_ref, w_ref):
    h = _rmsnorm(x, g_ref[...]).astype(BF16)
    half = w_ref.shape[1] // 2
    return (_dot(h, w_ref[:, :half]), _dot(h, w_ref[:, half:]))


def _kv_prompt_kernel(x_ref, g_ref, w_ref, kt_ref, vt_ref, k16_ref, v16_ref):
    k, v = _kv_body(x_ref[...], g_ref, w_ref)
    kt_ref[0] = k.T
    vt_ref[0] = v.T
    k16_ref[...] = k.astype(BF16)
    v16_ref[...] = v.astype(BF16)


def _kv_prompt(xp, batch, g, w):
    n_p, d = xp.shape
    t = n_p // batch
    tm = min(TOKEN_TILE, t)
    assert t % tm == 0
    nt = t // tm
    row_spec = pl.BlockSpec((tm, d), lambda b, i: (b * nt + i, 0))
    fm_spec = pl.BlockSpec((1, d, tm), lambda b, i: (b, 0, i))
    return pl.pallas_call(
        _kv_prompt_kernel, grid=(batch, nt),
        in_specs=[row_spec, _resident(g), _resident(w)],
        out_specs=[fm_spec, fm_spec, row_spec, row_spec],
        out_shape=[jax.ShapeDtypeStruct((batch, d, t), F32), jax.ShapeDtypeStruct((batch, d, t), F32),
                   jax.ShapeDtypeStruct((n_p, d), BF16), jax.ShapeDtypeStruct((n_p, d), BF16)],
        compiler_params=_params("arbitrary", "arbitrary"), name="kv_proj_prompt",
    )(xp, g, w)


def _proj_residual_body(x, a, w_ref):
    return (x + _dot(a.astype(BF16), w_ref[...]),)


def _ffn_ple_body(x, p, *rest, mixer_proj, final):
    rest = list(rest)
    mixed = rest.pop(0) if mixer_proj else None
    ffn_refs, ple_refs, tail = rest[:4], rest[4:7], rest[7:]
    if mixer_proj:
        x = x + _dot(mixed, tail.pop(0)[...])
    (x,) = _ffn_body(x, *ffn_refs)
    (x,) = _ple_body(x, p, *ple_refs)
    return (_rmsnorm(x, tail[0][...]) if final else x,)


def _ret_in_proj_body(x, g_ref, w_ref, *, qk, vd):
    h = _rmsnorm(x, g_ref[...]).astype(BF16)
    return (_dot(h, w_ref[:, :qk]), _dot(h, w_ref[:, qk:2 * qk]),
            _dot(h, w_ref[:, 2 * qk:2 * qk + vd]), _dot(h, w_ref[:, 2 * qk + vd:]))


def _rope_tables(pos, half):
    inv_freq = ROPE_BASE ** (-jnp.arange(half, dtype=F32) / half)
    ang = pos.astype(F32)[:, None] * inv_freq[None, :]
    return jnp.cos(ang), jnp.sin(ang)


def _rope(x, cos, sin):
    half = x.shape[-1] // 2
    x1, x2 = x[:, :half], x[:, half:]
    return jnp.concatenate([x1 * cos - x2 * sin, x2 * cos + x1 * sin], axis=-1)


def _log_decay(head):
    return math.log(1.0 - 2.0 ** (-5.0 - head))


def _group_norm_gate(o, g, ggn):
    mu = jnp.mean(o, axis=-1, keepdims=True)
    var = jnp.mean(jnp.square(o - mu), axis=-1, keepdims=True)
    on = (o - mu) * lax.rsqrt(var + EPS) * ggn
    return _silu(g) * on


def _ret_prompt_kernel(x_ref, gn_ref, win_ref, wout_ref, ggn_ref, cos_ref, sin_ref,
                       o_ref, s_ref, *, rows, dk, dv):
    qk, vd = RET_HEADS * dk, RET_HEADS * dv

    @pl.when(pl.program_id(1) == 0)
    def _():
        s_ref[...] = jnp.zeros_like(s_ref)

    x = x_ref[...]
    h = _rmsnorm(x, gn_ref[...]).astype(BF16)
    cos, sin = cos_ref[...], sin_ref[...]
    diff = (lax.broadcasted_iota(jnp.int32, (rows, rows), 0)
            - lax.broadcasted_iota(jnp.int32, (rows, rows), 1)).astype(F32)
    ri = lax.broadcasted_iota(jnp.int32, (rows, 1), 0).astype(F32)
    acc = None
    for hh in range(RET_HEADS):
        lg = _log_decay(hh)
        q = _rope(_dot(h, win_ref[:, hh * dk:(hh + 1) * dk]), cos, sin)
        k = _rope(_dot(h, win_ref[:, qk + hh * dk:qk + (hh + 1) * dk]), cos, sin) * (dk ** -0.5)
        v = _dot(h, win_ref[:, 2 * qk + hh * dv:2 * qk + (hh + 1) * dv]).astype(BF16)
        g = _dot(h, win_ref[:, 2 * qk + vd + hh * dv:2 * qk + vd + (hh + 1) * dv])
        decay = jnp.where(diff >= 0, jnp.exp(jnp.maximum(diff, 0.0) * lg), 0.0)
        s = _dot_nt(q.astype(BF16), k.astype(BF16)) * decay
        state = s_ref[0, hh]
        q_dec = (q * jnp.exp((ri + 1.0) * lg)).astype(BF16)
        o = _dot(s.astype(BF16), v) + _dot(q_dec, state.astype(BF16))
        k_dec = k * jnp.exp((rows - 1.0 - ri) * lg)
        s_ref[0, hh] = math.exp(rows * lg) * state + _dot(k_dec.T.astype(BF16), v)
        gated = _group_norm_gate(o, g, ggn_ref[:, hh * dv:(hh + 1) * dv]).astype(BF16)
        d = _dot(gated, wout_ref[hh * dv:(hh + 1) * dv, :])
        acc = d if acc is None else acc + d
    o_ref[...] = x + acc


def _retention_prompt(xp, batch, g, w_in, w_out, g_gn, dk, dv):
    n_p, d = xp.shape
    t = n_p // batch
    rows = min(RET_CHUNK_ROWS, t)
    assert t % rows == 0
    nch = t // rows
    cos, sin = _rope_tables(jnp.arange(t, dtype=jnp.int32), dk // 2)
    kern = functools.partial(_ret_prompt_kernel, rows=rows, dk=dk, dv=dv)
    return pl.pallas_call(
        kern, grid=(batch, nch),
        in_specs=[pl.BlockSpec((rows, d), lambda b, c: (b * nch + c, 0)),
                  _resident(g), _resident(w_in), _resident(w_out), _resident(g_gn),
                  pl.BlockSpec((rows, dk // 2), lambda b, c: (c, 0)),
                  pl.BlockSpec((rows, dk // 2), lambda b, c: (c, 0))],
        out_specs=[pl.BlockSpec((rows, d), lambda b, c: (b * nch + c, 0)),
                   pl.BlockSpec((1, RET_HEADS, dk, dv), lambda b, c: (b, 0, 0, 0))],
        out_shape=[jax.ShapeDtypeStruct((n_p, d), F32),
                   jax.ShapeDtypeStruct((batch, RET_HEADS, dk, dv), F32)],
        compiler_params=_params("arbitrary", "arbitrary"), name="retention_prompt",
    )(xp, g, _array(w_in), _array(w_out), g_gn, cos, sin)


def _pad_rows(x, rows):
    return jnp.concatenate([x, jnp.zeros((rows - x.shape[0], x.shape[1]), x.dtype)], axis=0)


def _ret_sample_kernel(q_ref, k_ref, v_ref, g_ref, ggn_ref, cos_ref, sin_ref, s_ref, *rest,
                       t_dec, dk, dv, n_fill):
    o_ref, so_ref = rest[-2], rest[-1]
    nrow = 2 * t_dec
    cos, sin = cos_ref[...], sin_ref[...]
    r_i = lax.broadcasted_iota(jnp.int32, (nrow, LANES), 0)
    c_i = lax.broadcasted_iota(jnp.int32, (nrow, LANES), 1)
    same_seq = (r_i // t_dec) == (c_i // t_dec)
    diff = ((r_i % t_dec) - (c_i % t_dec)).astype(F32)
    valid = same_seq & (diff >= 0) & (c_i < nrow)
    t_row = (lax.broadcasted_iota(jnp.int32, (nrow, 1), 0) % t_dec).astype(F32)
    seq_row = lax.broadcasted_iota(jnp.int32, (nrow, 1), 0) // t_dec
    seq_lane = lax.broadcasted_iota(jnp.int32, (1, LANES), 1) // t_dec
    for fill in range(n_fill):
        so_ref[1 + fill] = jnp.zeros(so_ref.shape[1:], so_ref.dtype)
    outs = []
    for hh in range(RET_HEADS):
        lg = _log_decay(hh)
        q = _rope(q_ref[0, :, hh * dk:(hh + 1) * dk], cos, sin)
        k = _rope(k_ref[0, :, hh * dk:(hh + 1) * dk], cos, sin) * (dk ** -0.5)
        v = _pad_rows(v_ref[0, :, hh * dv:(hh + 1) * dv], LANES).astype(BF16)
        g = g_ref[0, :, hh * dv:(hh + 1) * dv]
        decay = jnp.where(valid, jnp.exp(jnp.maximum(diff, 0.0) * lg), 0.0)
        q16 = _pad_rows(q, 16)
        s = _dot_nt(q16.astype(BF16), _pad_rows(k, LANES).astype(BF16))[:nrow] * decay
        o = _dot(_pad_rows(s, 16).astype(BF16), v)[:nrow]
        q_dec = _pad_rows(q * jnp.exp((t_row + 1.0) * lg), 16).astype(BF16)
        k_dec_t = _pad_rows(k * jnp.exp((t_dec - 1.0 - t_row) * lg), LANES).T
        for sq in range(2):
            state = s_ref[0, sq, hh]
            o_state = _dot(q_dec, state.astype(BF16))[:nrow]
            o = o + jnp.where(seq_row == sq, o_state, 0.0)
            k_sq = jnp.where(seq_lane == sq, k_dec_t, 0.0).astype(BF16)
            so_ref[0, sq, hh] = math.exp(t_dec * lg) * state + _dot(k_sq, v)
        outs.append(_group_norm_gate(o, g, ggn_ref[:, hh * dv:(hh + 1) * dv]))
    o_ref[0] = jnp.concatenate(outs, axis=-1)


def _retention_sample(q, k, v, g, g_gn, state_all, layer, prev_states, t_dec, past_len, dk, dv):
    n_s = q.shape[0]
    n_seq = n_s // t_dec
    n_layers = state_all.shape[0]
    assert n_seq % 2 == 0
    nrow = 2 * t_dec
    pos = past_len + (jnp.arange(nrow, dtype=jnp.int32) % t_dec)
    cos, sin = _rope_tables(pos, dk // 2)
    grp = lambda a: a.reshape(n_seq // 2, nrow, a.shape[1])
    qk, vd = RET_HEADS * dk, RET_HEADS * dv
    first = prev_states is None
    n_fill = n_layers - 1 if first else 0
    kern = functools.partial(_ret_sample_kernel, t_dec=t_dec, dk=dk, dv=dv, n_fill=n_fill)
    row_spec = lambda w: pl.BlockSpec((1, nrow, w), lambda i: (i, 0, 0))
    st_in = pl.BlockSpec((1, 2, RET_HEADS, dk, dv), lambda i: (layer, i, 0, 0, 0))
    st_out = pl.BlockSpec((1 + n_fill, 2, RET_HEADS, dk, dv), lambda i: (0 if first else layer, i, 0, 0, 0))
    in_specs = [row_spec(qk), row_spec(qk), row_spec(vd), row_spec(vd), _resident(g_gn),
                _resident(cos), _resident(sin), st_in]
    args = [grp(q), grp(k), grp(v), grp(g), g_gn, cos, sin, state_all]
    aliases = {}
    if not first:
        in_specs.append(pl.BlockSpec(memory_space=pl.ANY))
        args.append(prev_states)
        aliases = {len(args) - 1: 1}
    gated, s_new = pl.pallas_call(
        kern, grid=(n_seq // 2,), in_specs=in_specs,
        out_specs=[row_spec(vd), st_out],
        out_shape=[jax.ShapeDtypeStruct((n_seq // 2, nrow, vd), F32),
                   jax.ShapeDtypeStruct(state_all.shape, F32)],
        input_output_aliases=aliases,
        compiler_params=_params("arbitrary"), name="retention_sample",
    )(*args)
    return gated.reshape(n_s, vd), s_new


def _suffix_ones(n):
    r = lax.broadcasted_iota(jnp.int32, (n, n), 0)
    c = lax.broadcasted_iota(jnp.int32, (n, n), 1)
    return jnp.where(r >= c, 1.0, 0.0).astype(BF16)


def _sb_blocks(ys, ones):
    parts = [_split_bf16(_softplus2(y)) for y in ys]
    suffixes = [_dot(hi, ones) + _dot(lo, ones) for hi, lo in parts]
    return [(jnp.exp2(y - sfx), sfx[:, :1]) for y, sfx in zip(ys, suffixes)]


def _sb_blocks_skewed(n, logits, ones, use):
    def suffix(y):
        return _dot(_softplus2(y).astype(BF16), ones)

    ys, sfx, results = {}, {}, []
    for step in range(n + 2):
        if step < n:
            ys[step] = logits(step)
        if 1 <= step <= n:
            sfx[step - 1] = suffix(ys[step - 1])
        if step >= 2:
            j = step - 2
            results.append(use(j, jnp.exp2(ys[j] - sfx[j]), sfx[j][:, :1]))
    return results


def _sb_prompt_kernel(bias_ref, q_ref, k_ref, v_ref, o_ref, qh_ref, ones_ref, bm_ref, acc_ref, carry_ref,
                      *, tq, tk, hd, n_pairs):
    grp, qi = pl.program_id(1), pl.program_id(2)
    ratio = tq // tk
    lo_lanes = lax.broadcasted_iota(jnp.int32, (1, LANES), 1) < hd
    row = lax.broadcasted_iota(jnp.int32, (tq, tk), 0)
    col = lax.broadcasted_iota(jnp.int32, (tq, tk), 1)
    for p in range(n_pairs):
        qp = q_ref[:, p * LANES:(p + 1) * LANES]
        zero = jnp.zeros_like(qp)
        qh_ref[2 * p] = jnp.where(lo_lanes, qp, zero)
        qh_ref[2 * p + 1] = jnp.where(lo_lanes, zero, qp)

    @pl.when(qi == 0)
    def _():
        for h in range(2 * n_pairs):
            bias = bias_ref[grp * 2 * n_pairs + h]
            bm_ref[h, 0] = jnp.full((tq, tk), bias, F32)
            for d in range(ratio):
                bm_ref[h, 1 + d] = jnp.where(col + d * tk < row, bias, MASKED_LOGIT)
        ones_ref[...] = _suffix_ones(tk)

    acc_ref[...] = jnp.zeros_like(acc_ref)
    carry_ref[...] = jnp.zeros_like(carry_ref)
    n_blocks = ratio * qi + ratio
    n_heads = 2 * n_pairs

    def visit(top, count):
        ones = ones_ref[...]
        starts = [pl.multiple_of((top - u) * tk, tk) for u in range(count)]
        kinds = [jnp.clip(top - u - ratio * qi + 1, 0, ratio) for u in range(count)]

        def logits(i):
            u, h = divmod(i, n_heads)
            kb = k_ref[pl.ds(starts[u], tk), (h // 2) * LANES:(h // 2 + 1) * LANES]
            return _dot_nt(qh_ref[h], kb) + bm_ref[h, kinds[u]]

        def weigh(i, a, tot):
            u, h = divmod(i, n_heads)
            vb = v_ref[pl.ds(starts[u], tk), (h // 2) * LANES:(h // 2 + 1) * LANES]
            vzero = jnp.zeros_like(vb)
            vh = jnp.where(lo_lanes, vb, vzero) if h % 2 == 0 else jnp.where(lo_lanes, vzero, vb)
            return _dot(a.astype(BF16), vh), tot

        blocks = _sb_blocks_skewed(count * n_heads, logits, ones, weigh)
        for p in range(n_pairs):
            c = carry_ref[p]
            acc = acc_ref[:, p * LANES:(p + 1) * LANES]
            for u in range(count):
                (d0, t0), (d1, t1) = blocks[u * n_heads + 2 * p], blocks[u * n_heads + 2 * p + 1]
                acc = acc + jnp.exp2(-c) * (d0 + d1)
                c = c + jnp.where(lo_lanes, t0, t1)
            acc_ref[:, p * LANES:(p + 1) * LANES] = acc
            carry_ref[p] = c

    big = SB_BLOCKS_PER_VISIT
    rem = lax.rem(n_blocks, big)
    if big > ratio:
        @pl.when(rem != 0)
        def _():
            visit(n_blocks - 1, ratio)

    def body(it, _):
        visit(n_blocks - 1 - rem - it * big, big)
        return 0

    lax.fori_loop(0, n_blocks // big, body, 0)
    o_ref[...] = acc_ref[...].astype(o_ref.dtype)


def _sb_prompt(q, k, v, bias, batch, hd):
    n_p, d = q.shape
    t = n_p // batch
    tq, tk = min(SB_QUERY_BLOCK, t), min(SB_KEY_BLOCK, t)
    n_pairs = SB_HEADS_PER_STEP // 2
    width = n_pairs * LANES
    assert t % tq == 0 and tq % tk == 0 and 2 * hd == LANES and d % width == 0
    assert SB_BLOCKS_PER_VISIT % (tq // tk) == 0 and SB_BLOCKS_PER_VISIT <= 2 * (tq // tk)
    nq = t // tq
    kern = functools.partial(_sb_prompt_kernel, tq=tq, tk=tk, hd=hd, n_pairs=n_pairs)
    return pl.pallas_call(
        kern, grid=(batch, d // width, nq),
        in_specs=[pl.BlockSpec(memory_space=pltpu.SMEM),
                  pl.BlockSpec((tq, width), lambda b, g, i: (b * nq + i, g)),
                  pl.BlockSpec((t, width), lambda b, g, i: (b, g)),
                  pl.BlockSpec((t, width), lambda b, g, i: (b, g))],
        out_specs=pl.BlockSpec((tq, width), lambda b, g, i: (b * nq + i, g)),
        out_shape=jax.ShapeDtypeStruct((n_p, d), BF16),
        scratch_shapes=[pltpu.VMEM((2 * n_pairs, tq, LANES), BF16), pltpu.VMEM((tk, tk), BF16),
                        pltpu.VMEM((2 * n_pairs, 1 + tq // tk, tq, tk), F32), pltpu.VMEM((tq, width), F32),
                        pltpu.VMEM((n_pairs, tq, LANES), F32)],
        compiler_params=_params("arbitrary", "arbitrary", "arbitrary"), name="sb_prompt",
    )(bias, q, k, v)


def _sb_sample_kernel(pt_ref, q_ref, kn_ref, vn_ref, bias_ref, *refs, t_dec, hd, n_pages):
    del pt_ref
    kt_refs, vt_refs, o_ref = refs[:n_pages], refs[n_pages:2 * n_pages], refs[2 * n_pages]
    page = kt_refs[0].shape[2]
    nrow = t_dec * SB_HEADS
    d = SB_HEADS * hd
    blk = 2 * page
    head_mask = ((lax.broadcasted_iota(jnp.int32, (SB_HEADS, d), 1) // hd)
                 == lax.broadcasted_iota(jnp.int32, (SB_HEADS, d), 0))
    bias = bias_ref[:, :1]
    qv = q_ref[0]
    qbd = jnp.concatenate([jnp.where(head_mask, qv[t:t + 1, :], 0.0) for t in range(t_dec)]
                          + [jnp.zeros((LANES - nrow, d), F32)], axis=0).astype(BF16)
    key_i = lax.broadcasted_iota(jnp.int32, (LANES, blk), 1)
    qry_t = lax.broadcasted_iota(jnp.int32, (LANES, blk), 0) // SB_HEADS
    y_new = _dot_nt(qbd, _pad_rows(kn_ref[0], blk).astype(BF16)) + bias
    y_pages = [_dot(qbd, kt[0].astype(BF16)) for kt in kt_refs]
    ys = [jnp.where(key_i < qry_t, y_new, MASKED_LOGIT)]
    ys += [jnp.concatenate([y_pages[2 * i], y_pages[2 * i + 1]], axis=1) + bias for i in reversed(range(n_pages // 2))]
    blocks = _sb_blocks(ys, _suffix_ones(blk))
    carry = jnp.zeros((LANES, 1), F32)
    weights = []
    for a, tot in blocks:
        weights.append((a * jnp.exp2(-carry)).astype(BF16))
        carry = carry + tot
    out_new = _dot(weights[0], _pad_rows(vn_ref[0], blk).astype(BF16))
    out_t = None
    for w, i in zip(weights[1:], reversed(range(n_pages // 2))):
        v_t = jnp.concatenate([vt_refs[2 * i][0], vt_refs[2 * i + 1][0]], axis=1).astype(BF16)
        part = _dot_nt(v_t, w)
        out_t = part if out_t is None else out_t + part
    rows = [jnp.sum(jnp.where(head_mask, out_new[t * SB_HEADS:(t + 1) * SB_HEADS, :], 0.0), axis=0, keepdims=True)
            for t in range(t_dec)]
    f_head = lax.broadcasted_iota(jnp.int32, (d, LANES), 0) // hd
    r_idx = lax.broadcasted_iota(jnp.int32, (d, LANES), 1)
    hi, lo = _split_bf16(jnp.where((r_idx % SB_HEADS == f_head) & (r_idx < nrow), out_t, 0.0))
    sel = jnp.where(lax.broadcasted_iota(jnp.int32, (16, LANES), 1) // SB_HEADS
                    == lax.broadcasted_iota(jnp.int32, (16, LANES), 0), 1.0, 0.0).astype(BF16)
    past = _dot_nt(sel, hi) + _dot_nt(sel, lo)
    o_ref[0] = jnp.concatenate(rows, axis=0) + past[:t_dec]


def _feature_major_pages(cache):
    n_pool, page, heads, hd = cache.shape
    return jnp.transpose(cache, (0, 2, 3, 1)).reshape(n_pool, heads * hd, page)


def _sb_sample(q, k_new, v_new, pages_kt, pages_vt, page_table, bias, t_dec, hd):
    n_s, d = q.shape
    n_seq = n_s // t_dec
    page = pages_kt.shape[2]
    n_pages = page_table.shape[1]
    nrow = t_dec * SB_HEADS
    assert page == LANES and nrow <= LANES and n_pages % 2 == 0 and t_dec <= 16
    bias_rows = jnp.broadcast_to(jnp.tile(bias.astype(F32), t_dec)[:, None], (nrow, LANES))
    bias_rows = jnp.concatenate([bias_rows, jnp.zeros((LANES - nrow, LANES), F32)], axis=0)
    seq3 = lambda a: a.reshape(n_seq, t_dec, d)
    seq_spec = pl.BlockSpec((1, t_dec, d), lambda b, pt: (b, 0, 0))

    def page_spec(j):
        return pl.BlockSpec((1, d, page), lambda b, pt: (pt[b * n_pages + j], 0, 0))

    kern = functools.partial(_sb_sample_kernel, t_dec=t_dec, hd=hd, n_pages=n_pages)
    out = pl.pallas_call(
        kern,
        grid_spec=pltpu.PrefetchScalarGridSpec(
            num_scalar_prefetch=1, grid=(n_seq,),
            in_specs=[seq_spec, seq_spec, seq_spec, pl.BlockSpec((LANES, LANES), lambda b, pt: (0, 0))]
                     + [page_spec(j) for j in range(n_pages)] + [page_spec(j) for j in range(n_pages)],
            out_specs=seq_spec),
        out_shape=jax.ShapeDtypeStruct((n_seq, t_dec, d), F32),
        compiler_params=_params("arbitrary"), name="sb_sample",
    )(page_table.reshape(-1), seq3(q), seq3(k_new), seq3(v_new), bias_rows,
      *([pages_kt] * n_pages), *([pages_vt] * n_pages))
    return out.reshape(n_s, d)


def kernel(x_prompt, x_sample, p_prompt, p_sample, state_ret, cache_k, cache_v, page_table, g_norm, w_ff1_gate, w_ff1_up, w_ff1_down, w_ff2_gate, w_ff2_up, w_ff2_down, w_ple_up, w_ple_gate, g_ple, w_ret_in, w_ret_out, g_ret_gn, g_kv, w_kv, w_sb_q, w_sb_out, b_sb, g_final):
    batch, t_prompt, d = x_prompt.shape
    n_dec, t_dec, _ = x_sample.shape
    depth = g_norm.shape[0]
    n_ret = w_ret_in.shape[0]
    dk = d // RET_HEADS
    dv = w_ret_out.shape[1] // RET_HEADS
    hd = d // SB_HEADS
    past_len = page_table.shape[1] * cache_k.shape[1]
    row = lambda a: a.reshape(1, -1)
    bf = lambda a: a.astype(BF16)

    xp = x_prompt.reshape(batch * t_prompt, d)
    xs = x_sample.reshape(n_dec * t_dec, d)
    x_out = [(d, F32)]
    states_p, states_s = [], None
    pages_kt, pages_vt = _feature_major_pages(cache_k), _feature_major_pages(cache_v)
    ff1 = [bf(w_ff1_gate), bf(w_ff1_up), bf(w_ff1_down)]
    ff2 = [bf(w_ff2_gate), bf(w_ff2_up), bf(w_ff2_down)]
    ple = [bf(w_ple_up), bf(w_ple_gate)]
    ret_in, ret_out, sb_q, sb_out = bf(w_ret_in), bf(w_ret_out), bf(w_sb_q), bf(w_sb_out)
    p_p = p_prompt.reshape(depth, batch * t_prompt, -1)
    p_s = p_sample.reshape(depth, n_dec * t_dec, -1)
    for i in range(depth):
        ffn1_consts = [row(g_norm[i, 0])] + [_Layer(w, i) for w in ff1]
        streams = [(xp, xs), (_Layer(p_p, i), _Layer(p_s, i))]
        if i < n_ret:
            (xp, xs), = _tokenwise(_ffn_body, [(xp, xs)], ffn1_consts, x_out, "ffn1")
            g_gn = row(g_ret_gn[i])
            xp, s_p = _retention_prompt(xp, batch, row(g_norm[i, 1]), _Layer(ret_in, i), _Layer(ret_out, i),
                                        g_gn, dk, dv)
            q, k, v, g = _rowwise(functools.partial(_ret_in_proj_body, qk=RET_HEADS * dk, vd=RET_HEADS * dv),
                                  [xs], [row(g_norm[i, 1]), _Layer(ret_in, i)],
                                  [(RET_HEADS * dk, F32), (RET_HEADS * dk, F32),
                                   (RET_HEADS * dv, F32), (RET_HEADS * dv, F32)], "ret_in_proj_sample")
            gated, states_s = _retention_sample(q, k, v, g, g_gn, state_ret, i, states_s, t_dec, past_len, dk, dv)
            xs, = _rowwise(_proj_residual_body, [xs, gated], [_Layer(ret_out, i)], x_out, "ret_out_proj_sample")
            states_p.append(s_p)
            streams[0] = (xp, xs)
            mixer_out = []
        else:
            l = i - n_ret
            (xp, xs), (qp, qs) = _tokenwise(functools.partial(_ffn_q_body, scale=LOG2E * hd ** -0.5), [(xp, xs)],
                                            ffn1_consts + [row(g_norm[i, 1]), _Layer(sb_q, l)],
                                            x_out + [(d, BF16)], "ffn1_q")
            bias2 = b_sb[l].astype(F32) * LOG2E
            o_p = _sb_prompt(qp, kp16, vp16, bias2, batch, hd)
            o_s = _sb_sample(qs.astype(F32), ks32, vs32, pages_kt, pages_vt, page_table, bias2, t_dec, hd)
            streams = [(xp, xs), streams[1], (o_p, bf(o_s))]
            mixer_out = [_Layer(sb_out, l)]
        last = i == depth - 1
        consts = ([row(g_norm[i, 2])] + [_Layer(w, i) for w in ff2] + [row(g_ple[i])] + [_Layer(w, i) for w in ple]
                  + mixer_out + ([row(g_final)] if last else []))
        (xp, xs), = _tokenwise(functools.partial(_ffn_ple_body, mixer_proj=bool(mixer_out), final=last),
                               streams, consts, x_out, "ffn2_ple")
        if i == n_ret - 1:
            kpt, vpt, kp16, vp16 = _kv_prompt(xp, batch, row(g_kv), bf(w_kv))
            ks32, vs32 = _rowwise(_kv_body, [xs], [row(g_kv), bf(w_kv)], [(d, F32), (d, F32)], "kv_proj_sample")
    yp, ys = xp, xs
    heads_last = lambda a: jnp.transpose(a.reshape(batch, SB_HEADS, hd, t_prompt), (0, 3, 1, 2))
    return (yp.reshape(batch, t_prompt, d), ys.reshape(n_dec, t_dec, d),
            jnp.stack(states_p), states_s,
            heads_last(kpt), heads_last(vpt),
            ks32.reshape(n_dec, t_dec, SB_HEADS, hd), vs32.reshape(n_dec, t_dec, SB_HEADS, hd))
```

```python
import functools
import math

import jax
import jax.numpy as jnp
from jax import lax
from jax.experimental import pallas as pl
from jax.experimental.pallas import tpu as pltpu

F32 = jnp.float32
BF16 = jnp.bfloat16

EPS = 1e-6
ROPE_BASE = 10000.0
RET_HEADS = 4
SB_HEADS = 16
LANES = 128
VMEM_LIMIT_BYTES = 58 * 1024 * 1024
MASKED_LOGIT = -1e30
LOG2E = 1.4426950408889634
TOKEN_TILE = 512
RET_CHUNK_ROWS = 512
SB_QUERY_BLOCK = 512
SB_KEY_BLOCK = 256
SB_HEADS_PER_STEP = 4
SB_BLOCKS_PER_VISIT = 4


def _params(*sem, flags=None):
    return pltpu.CompilerParams(dimension_semantics=sem, vmem_limit_bytes=VMEM_LIMIT_BYTES, flags=flags)


def _dot(a, b):
    return jnp.dot(a, b, preferred_element_type=F32)


def _dot_nt(a, b):
    return lax.dot_general(a, b, (((1,), (1,)), ((), ())), preferred_element_type=F32)


def _rmsnorm(x, g):
    y = x * lax.rsqrt(jnp.mean(x * x, axis=-1, keepdims=True) + EPS)
    return y * g


def _silu(x):
    return x * jax.nn.sigmoid(x)


def _softplus2(y):
    neg_abs = lax.bitcast_convert_type(lax.bitcast_convert_type(y, jnp.int32) | jnp.int32(-2 ** 31), F32)
    return jnp.maximum(y, 0.0) + jnp.log(1.0 + jnp.exp2(neg_abs)) * LOG2E


def _split_bf16(x):
    hi = x.astype(BF16)
    lo = (x - hi.astype(F32)).astype(BF16)
    return hi, lo


class _Layer:
    def __init__(self, stack, layer):
        self.stack, self.layer = stack, layer
        self.shape = stack.shape[1:]


def _array(a):
    return a.stack if isinstance(a, _Layer) else a


def _resident(arr):
    nd = len(arr.shape)
    if isinstance(arr, _Layer):
        return pl.BlockSpec((None,) + tuple(arr.shape), lambda *_: (arr.layer,) + (0,) * nd,
                            pipeline_mode=pl.Buffered(1))
    return pl.BlockSpec(arr.shape, lambda *_: (0,) * nd, pipeline_mode=pl.Buffered(1))


def _tokenwise(body, streams, consts, outs, name):
    n_p, n_s = streams[0][0].shape[0], streams[0][1].shape[0]
    tm = min(TOKEN_TILE, n_s)
    assert n_p % tm == 0 and n_s % tm == 0
    p_tiles, s_tiles = n_p // tm, n_s // tm
    ns, nc, no = len(streams), len(consts), len(outs)

    def kern(*refs):
        in_refs, c_refs, o_refs = refs[:2 * ns], refs[2 * ns:2 * ns + nc], refs[2 * ns + nc:]
        is_prompt = pl.program_id(0) < p_tiles
        vals = [jnp.where(is_prompt, in_refs[2 * k][...], in_refs[2 * k + 1][...]) for k in range(ns)]
        res = body(*vals, *c_refs)

        @pl.when(is_prompt)
        def _():
            for k in range(no):
                o_refs[2 * k][...] = res[k].astype(outs[k][1])

        @pl.when(jnp.logical_not(is_prompt))
        def _():
            for k in range(no):
                o_refs[2 * k + 1][...] = res[k].astype(outs[k][1])

    def p_spec(w, layer=None):
        if layer is not None:
            return pl.BlockSpec((None, tm, w), lambda i: (layer, jnp.minimum(i, p_tiles - 1), 0))
        return pl.BlockSpec((tm, w), lambda i: (jnp.minimum(i, p_tiles - 1), 0))

    def s_spec(w, layer=None):
        if layer is not None:
            return pl.BlockSpec((None, tm, w), lambda i: (layer, jnp.maximum(i - p_tiles, 0), 0))
        return pl.BlockSpec((tm, w), lambda i: (jnp.maximum(i - p_tiles, 0), 0))

    in_specs, args = [], []
    for a_p, a_s in streams:
        layer = a_p.layer if isinstance(a_p, _Layer) else None
        in_specs += [p_spec(a_p.shape[1], layer), s_spec(a_s.shape[1], layer)]
        args += [_array(a_p), _array(a_s)]
    in_specs += [_resident(c) for c in consts]
    consts = [_array(c) for c in consts]
    out_specs, out_shape = [], []
    for w, dt in outs:
        out_specs += [p_spec(w), s_spec(w)]
        out_shape += [jax.ShapeDtypeStruct((n_p, w), dt), jax.ShapeDtypeStruct((n_s, w), dt)]
    res = pl.pallas_call(
        kern, grid=(p_tiles + s_tiles,), in_specs=in_specs, out_specs=out_specs,
        out_shape=out_shape, compiler_params=_params("arbitrary"), name=name,
    )(*args, *consts)
    return [(res[2 * k], res[2 * k + 1]) for k in range(no)]


def _rowwise(body, rows, consts, outs, name):
    n = rows[0].shape[0]
    tm = min(TOKEN_TILE, n)
    assert n % tm == 0
    nr, nc, no = len(rows), len(consts), len(outs)

    def kern(*refs):
        res = body(*[r[...] for r in refs[:nr]], *refs[nr:nr + nc])
        for k in range(no):
            refs[nr + nc + k][...] = res[k].astype(outs[k][1])

    in_specs = [pl.BlockSpec((tm, a.shape[1]), lambda i: (i, 0)) for a in rows]
    in_specs += [_resident(c) for c in consts]
    consts = [_array(c) for c in consts]
    return pl.pallas_call(
        kern, grid=(n // tm,), in_specs=in_specs,
        out_specs=[pl.BlockSpec((tm, w), lambda i: (i, 0)) for w, _ in outs],
        out_shape=[jax.ShapeDtypeStruct((n, w), dt) for w, dt in outs],
        compiler_params=_params("arbitrary"), name=name,
    )(*rows, *consts)


def _ffn_body(x, g_ref, wg_ref, wu_ref, wd_ref):
    h = _rmsnorm(x, g_ref[...]).astype(BF16)
    d_ff = wg_ref.shape[1]
    chunk = 256 if d_ff % 256 == 0 else LANES
    acc = None
    for c0 in range(0, d_ff, chunk):
        gate = _dot(h, wg_ref[:, c0:c0 + chunk])
        up = _dot(h, wu_ref[:, c0:c0 + chunk])
        d = _dot((_silu(gate) * up).astype(BF16), wd_ref[c0:c0 + chunk, :])
        acc = d if acc is None else acc + d
    return (x + 0.5 * acc,)


def _ple_body(x, p, g_ref, wup_ref, wgate_ref):
    h = _rmsnorm(x, g_ref[...]).astype(BF16)
    gate = jax.nn.sigmoid(_dot(h, wgate_ref[...]))
    return (x + _dot(p.astype(BF16), wup_ref[...]) * gate,)


def _ffn_q_body(x, g_ref, wg_ref, wu_ref, wd_ref, gq_ref, wq_ref, *, scale):
    (x,) = _ffn_body(x, g_ref, wg_ref, wu_ref, wd_ref)
    return (x, _dot(_rmsnorm(x, gq_ref[...]).astype(BF16), wq_ref[...]) * scale)


def _kv_body(x, g_ref, w_ref):
    h = _rmsnorm(x, g_ref[...]).astype(BF16)
    half = w_ref.shape[1] // 2
    return (_dot(h, w_ref[:, :half]), _dot(h, w_ref[:, half:]))


def _kv_prompt_kernel(x_ref, g_ref, w_ref, kt_ref, vt_ref, k16_ref, v16_ref):
    k, v = _kv_body(x_ref[...], g_ref, w_ref)
    kt_ref[0] = k.T
    vt_ref[0] = v.T
    k16_ref[...] = k.astype(BF16)
    v16_ref[...] = v.astype(BF16)


def _kv_prompt(xp, batch, g, w):
    n_p, d = xp.shape
    t = n_p // batch
    tm = min(TOKEN_TILE, t)
    assert t % tm == 0
    nt = t // tm
    row_spec = pl.BlockSpec((tm, d), lambda b, i: (b * nt + i, 0))
    fm_spec = pl.BlockSpec((1, d, tm), lambda b, i: (b, 0, i))
    return pl.pallas_call(
        _kv_prompt_kernel, grid=(batch, nt),
        in_specs=[row_spec, _resident(g), _resident(w)],
        out_specs=[fm_spec, fm_spec, row_spec, row_spec],
        out_shape=[jax.ShapeDtypeStruct((batch, d, t), F32), jax.ShapeDtypeStruct((batch, d, t), F32),
                   jax.ShapeDtypeStruct((n_p, d), BF16), jax.ShapeDtypeStruct((n_p, d), BF16)],
        compiler_params=_params("arbitrary", "arbitrary"), name="kv_proj_prompt",
    )(xp, g, w)


def _proj_residual_body(x, a, w_ref):
    return (x + _dot(a.astype(BF16), w_ref[...]),)


def _ffn_ple_body(x, p, *rest, mixer_proj, final):
    rest = list(rest)
    mixed = rest.pop(0) if mixer_proj else None
    ffn_refs, ple_refs, tail = rest[:4], rest[4:7], rest[7:]
    if mixer_proj:
        x = x + _dot(mixed, tail.pop(0)[...])
    (x,) = _ffn_body(x, *ffn_refs)
    (x,) = _ple_body(x, p, *ple_refs)
    return (_rmsnorm(x, tail[0][...]) if final else x,)


def _ret_in_proj_body(x, g_ref, w_ref, *, qk, vd):
    h = _rmsnorm(x, g_ref[...]).astype(BF16)
    return (_dot(h, w_ref[:, :qk]), _dot(h, w_ref[:, qk:2 * qk]),
            _dot(h, w_ref[:, 2 * qk:2 * qk + vd]), _dot(h, w_ref[:, 2 * qk + vd:]))


def _rope_tables(pos, half):
    inv_freq = ROPE_BASE ** (-jnp.arange(half, dtype=F32) / half)
    ang = pos.astype(F32)[:, None] * inv_freq[None, :]
    return jnp.cos(ang), jnp.sin(ang)


def _rope(x, cos, sin):
    half = x.shape[-1] // 2
    x1, x2 = x[:, :half], x[:, half:]
    return jnp.concatenate([x1 * cos - x2 * sin, x2 * cos + x1 * sin], axis=-1)


def _log_decay(head):
    return math.log(1.0 - 2.0 ** (-5.0 - head))


def _group_norm_gate(o, g, ggn):
    mu = jnp.mean(o, axis=-1, keepdims=True)
    var = jnp.mean(jnp.square(o - mu), axis=-1, keepdims=True)
    on = (o - mu) * lax.rsqrt(var + EPS) * ggn
    return _silu(g) * on


def _ret_prompt_kernel(x_ref, gn_ref, win_ref, wout_ref, ggn_ref, cos_ref, sin_ref,
                       o_ref, s_ref, *, rows, dk, dv):
    qk, vd = RET_HEADS * dk, RET_HEADS * dv

    @pl.when(pl.program_id(1) == 0)
    def _():
        s_ref[...] = jnp.zeros_like(s_ref)

    x = x_ref[...]
    h = _rmsnorm(x, gn_ref[...]).astype(BF16)
    cos, sin = cos_ref[...], sin_ref[...]
    diff = (lax.broadcasted_iota(jnp.int32, (rows, rows), 0)
            - lax.broadcasted_iota(jnp.int32, (rows, rows), 1)).astype(F32)
    ri = lax.broadcasted_iota(jnp.int32, (rows, 1), 0).astype(F32)
    acc = None
    for hh in range(RET_HEADS):
        lg = _log_decay(hh)
        q = _rope(_dot(h, win_ref[:, hh * dk:(hh + 1) * dk]), cos, sin)
        k = _rope(_dot(h, win_ref[:, qk + hh * dk:qk + (hh + 1) * dk]), cos, sin) * (dk ** -0.5)
        v = _dot(h, win_ref[:, 2 * qk + hh * dv:2 * qk + (hh + 1) * dv]).astype(BF16)
        g = _dot(h, win_ref[:, 2 * qk + vd + hh * dv:2 * qk + vd + (hh + 1) * dv])
        decay = jnp.where(diff >= 0, jnp.exp(jnp.maximum(diff, 0.0) * lg), 0.0)
        s = _dot_nt(q.astype(BF16), k.astype(BF16)) * decay
        state = s_ref[0, hh]
        q_dec = (q * jnp.exp((ri + 1.0) * lg)).astype(BF16)
        o = _dot(s.astype(BF16), v) + _dot(q_dec, state.astype(BF16))
        k_dec = k * jnp.exp((rows - 1.0 - ri) * lg)
        s_ref[0, hh] = math.exp(rows * lg) * state + _dot(k_dec.T.astype(BF16), v)
        gated = _group_norm_gate(o, g, ggn_ref[:, hh * dv:(hh + 1) * dv]).astype(BF16)
        d = _dot(gated, wout_ref[hh * dv:(hh + 1) * dv, :])
        acc = d if acc is None else acc + d
    o_ref[...] = x + acc


def _retention_prompt(xp, batch, g, w_in, w_out, g_gn, dk, dv):
    n_p, d = xp.shape
    t = n_p // batch
    rows = min(RET_CHUNK_ROWS, t)
    assert t % rows == 0
    nch = t // rows
    cos, sin = _rope_tables(jnp.arange(t, dtype=jnp.int32), dk // 2)
    kern = functools.partial(_ret_prompt_kernel, rows=rows, dk=dk, dv=dv)
    return pl.pallas_call(
        kern, grid=(batch, nch),
        in_specs=[pl.BlockSpec((rows, d), lambda b, c: (b * nch + c, 0)),
                  _resident(g), _resident(w_in), _resident(w_out), _resident(g_gn),
                  pl.BlockSpec((rows, dk // 2), lambda b, c: (c, 0)),
                  pl.BlockSpec((rows, dk // 2), lambda b, c: (c, 0))],
        out_specs=[pl.BlockSpec((rows, d), lambda b, c: (b * nch + c, 0)),
                   pl.BlockSpec((1, RET_HEADS, dk, dv), lambda b, c: (b, 0, 0, 0))],
        out_shape=[jax.ShapeDtypeStruct((n_p, d), F32),
                   jax.ShapeDtypeStruct((batch, RET_HEADS, dk, dv), F32)],
        compiler_params=_params("arbitrary", "arbitrary"), name="retention_prompt",
    )(xp, g, _array(w_in), _array(w_out), g_gn, cos, sin)


def _pad_rows(x, rows):
    return jnp.concatenate([x, jnp.zeros((rows - x.shape[0], x.shape[1]), x.dtype)], axis=0)


def _ret_sample_kernel(q_ref, k_ref, v_ref, g_ref, ggn_ref, cos_ref, sin_ref, s_ref, *rest,
                       t_dec, dk, dv, n_fill):
    o_ref, so_ref = rest[-2], rest[-1]
    nrow = 2 * t_dec
    cos, sin = cos_ref[...], sin_ref[...]
    r_i = lax.broadcasted_iota(jnp.int32, (nrow, LANES), 0)
    c_i = lax.broadcasted_iota(jnp.int32, (nrow, LANES), 1)
    same_seq = (r_i // t_dec) == (c_i // t_dec)
    diff = ((r_i % t_dec) - (c_i % t_dec)).astype(F32)
    valid = same_seq & (diff >= 0) & (c_i < nrow)
    t_row = (lax.broadcasted_iota(jnp.int32, (nrow, 1), 0) % t_dec).astype(F32)
    seq_row = lax.broadcasted_iota(jnp.int32, (nrow, 1), 0) // t_dec
    seq_lane = lax.broadcasted_iota(jnp.int32, (1, LANES), 1) // t_dec
    for fill in range(n_fill):
        so_ref[1 + fill] = jnp.zeros(so_ref.shape[1:], so_ref.dtype)
    outs = []
    for hh in range(RET_HEADS):
        lg = _log_decay(hh)
        q = _rope(q_ref[0, :, hh * dk:(hh + 1) * dk], cos, sin)
        k = _rope(k_ref[0, :, hh * dk:(hh + 1) * dk], cos, sin) * (dk ** -0.5)
        v = _pad_rows(v_ref[0, :, hh * dv:(hh + 1) * dv], LANES).astype(BF16)
        g = g_ref[0, :, hh * dv:(hh + 1) * dv]
        decay = jnp.where(valid, jnp.exp(jnp.maximum(diff, 0.0) * lg), 0.0)
        q16 = _pad_rows(q, 16)
        s = _dot_nt(q16.astype(BF16), _pad_rows(k, LANES).astype(BF16))[:nrow] * decay
        o = _dot(_pad_rows(s, 16).astype(BF16), v)[:nrow]
        q_dec = _pad_rows(q * jnp.exp((t_row + 1.0) * lg), 16).astype(BF16)
        k_dec_t = _pad_rows(k * jnp.exp((t_dec - 1.0 - t_row) * lg), LANES).T
        for sq in range(2):
            state = s_ref[0, sq, hh]
            o_state = _dot(q_dec, state.astype(BF16))[:nrow]
            o = o + jnp.where(seq_row == sq, o_state, 0.0)
            k_sq = jnp.where(seq_lane == sq, k_dec_t, 0.0).astype(BF16)
            so_ref[0, sq, hh] = math.exp(t_dec * lg) * state + _dot(k_sq, v)
        outs.append(_group_norm_gate(o, g, ggn_ref[:, hh * dv:(hh + 1) * dv]))
    o_ref[0] = jnp.concatenate(outs, axis=-1)


def _retention_sample(q, k, v, g, g_gn, state_all, layer, prev_states, t_dec, past_len, dk, dv):
    n_s = q.shape[0]
    n_seq = n_s // t_dec
    n_layers = state_all.shape[0]
    assert n_seq % 2 == 0
    nrow = 2 * t_dec
    pos = past_len + (jnp.arange(nrow, dtype=jnp.int32) % t_dec)
    cos, sin = _rope_tables(pos, dk // 2)
    grp = lambda a: a.reshape(n_seq // 2, nrow, a.shape[1])
    qk, vd = RET_HEADS * dk, RET_HEADS * dv
    first = prev_states is None
    n_fill = n_layers - 1 if first else 0
    kern = functools.partial(_ret_sample_kernel, t_dec=t_dec, dk=dk, dv=dv, n_fill=n_fill)
    row_spec = lambda w: pl.BlockSpec((1, nrow, w), lambda i: (i, 0, 0))
    st_in = pl.BlockSpec((1, 2, RET_HEADS, dk, dv), lambda i: (layer, i, 0, 0, 0))
    st_out = pl.BlockSpec((1 + n_fill, 2, RET_HEADS, dk, dv), lambda i: (0 if first else layer, i, 0, 0, 0))
    in_specs = [row_spec(qk), row_spec(qk), row_spec(vd), row_spec(vd), _resident(g_gn),
                _resident(cos), _resident(sin), st_in]
    args = [grp(q), grp(k), grp(v), grp(g), g_gn, cos, sin, state_all]
    aliases = {}
    if not first:
        in_specs.append(pl.BlockSpec(memory_space=pl.ANY))
        args.append(prev_states)
        aliases = {len(args) - 1: 1}
    gated, s_new = pl.pallas_call(
        kern, grid=(n_seq // 2,), in_specs=in_specs,
        out_specs=[row_spec(vd), st_out],
        out_shape=[jax.ShapeDtypeStruct((n_seq // 2, nrow, vd), F32),
                   jax.ShapeDtypeStruct(state_all.shape, F32)],
        input_output_aliases=aliases,
        compiler_params=_params("arbitrary"), name="retention_sample",
    )(*args)
    return gated.reshape(n_s, vd), s_new


def _suffix_ones(n):
    r = lax.broadcasted_iota(jnp.int32, (n, n), 0)
    c = lax.broadcasted_iota(jnp.int32, (n, n), 1)
    return jnp.where(r >= c, 1.0, 0.0).astype(BF16)


def _sb_blocks(ys, ones):
    parts = [_split_bf16(_softplus2(y)) for y in ys]
    suffixes = [_dot(hi, ones) + _dot(lo, ones) for hi, lo in parts]
    return [(jnp.exp2(y - sfx), sfx[:, :1]) for y, sfx in zip(ys, suffixes)]


def _sb_blocks_skewed(n, logits, ones, use):
    def suffix(y):
        return _dot(_softplus2(y).astype(BF16), ones)

    ys, sfx, results = {}, {}, []
    for step in range(n + 2):
        if step < n:
            ys[step] = logits(step)
        if 1 <= step <= n:
            sfx[step - 1] = suffix(ys[step - 1])
        if step >= 2:
            j = step - 2
            results.append(use(j, jnp.exp2(ys[j] - sfx[j]), sfx[j][:, :1]))
    return results


def _sb_prompt_kernel(bias_ref, q_ref, k_ref, v_ref, o_ref, qh_ref, ones_ref, bm_ref, acc_ref, carry_ref,
                      *, tq, tk, hd, n_pairs):
    grp, qi = pl.program_id(1), pl.program_id(2)
    ratio = tq // tk
    lo_lanes = lax.broadcasted_iota(jnp.int32, (1, LANES), 1) < hd
    row = lax.broadcasted_iota(jnp.int32, (tq, tk), 0)
    col = lax.broadcasted_iota(jnp.int32, (tq, tk), 1)
    for p in range(n_pairs):
        qp = q_ref[:, p * LANES:(p + 1) * LANES]
        zero = jnp.zeros_like(qp)
        qh_ref[2 * p] = jnp.where(lo_lanes, qp, zero)
        qh_ref[2 * p + 1] = jnp.where(lo_lanes, zero, qp)

    @pl.when(qi == 0)
    def _():
        for h in range(2 * n_pairs):
            bm_ref[h] = jnp.where(col < row, bias_ref[grp * 2 * n_pairs + h], MASKED_LOGIT)
        ones_ref[...] = _suffix_ones(tk)

    acc_ref[...] = jnp.zeros_like(acc_ref)
    carry_ref[...] = jnp.zeros_like(carry_ref)
    n_heads = 2 * n_pairs
    past = ratio * qi

    def visit(top, count, overlap):
        ones = ones_ref[...]
        starts = [pl.multiple_of((top - u) * tk, tk) for u in range(count)]
        row0 = [(ratio - 1 - u) * tk if overlap else 0 for u in range(count)]
        biases = [bias_ref[grp * n_heads + h] for h in range(n_heads)]

        def logits(i):
            u, h = divmod(i, n_heads)
            kb = k_ref[pl.ds(starts[u], tk), (h // 2) * LANES:(h // 2 + 1) * LANES]
            s = _dot_nt(qh_ref[h, row0[u]:, :], kb)
            return s + (bm_ref[h, :tq - row0[u], :] if overlap else biases[h])

        def weigh(i, a, tot):
            u, h = divmod(i, n_heads)
            vb = v_ref[pl.ds(starts[u], tk), (h // 2) * LANES:(h // 2 + 1) * LANES]
            vzero = jnp.zeros_like(vb)
            vh = jnp.where(lo_lanes, vb, vzero) if h % 2 == 0 else jnp.where(lo_lanes, vzero, vb)
            return _dot(a.astype(BF16), vh), tot

        blocks = _sb_blocks_skewed(count * n_heads, logits, ones, weigh)
        for p in range(n_pairs):
            lanes = slice(p * LANES, (p + 1) * LANES)
            if overlap:
                for u in range(count):
                    (d0, t0), (d1, t1) = blocks[u * n_heads + 2 * p], blocks[u * n_heads + 2 * p + 1]
                    c = carry_ref[p, row0[u]:, :]
                    acc_ref[row0[u]:, lanes] += jnp.exp2(-c) * (d0 + d1)
                    carry_ref[p, row0[u]:, :] = c + jnp.where(lo_lanes, t0, t1)
            else:
                c = carry_ref[p]
                acc = acc_ref[:, lanes]
                for u in range(count):
                    (d0, t0), (d1, t1) = blocks[u * n_heads + 2 * p], blocks[u * n_heads + 2 * p + 1]
                    acc = acc + jnp.exp2(-c) * (d0 + d1)
                    c = c + jnp.where(lo_lanes, t0, t1)
                acc_ref[:, lanes] = acc
                carry_ref[p] = c

    visit(past + ratio - 1, ratio, True)
    big = SB_BLOCKS_PER_VISIT
    rem = lax.rem(past, big)
    if big > ratio:
        @pl.when(rem != 0)
        def _():
            visit(past - 1, ratio, False)

    def body(it, _):
        visit(past - 1 - rem - it * big, big, False)
        return 0

    lax.fori_loop(0, past // big, body, 0)
    o_ref[...] = acc_ref[...].astype(o_ref.dtype)


def _sb_prompt(q, k, v, bias, batch, hd):
    n_p, d = q.shape
    t = n_p // batch
    tq, tk = min(SB_QUERY_BLOCK, t), min(SB_KEY_BLOCK, t)
    n_pairs = SB_HEADS_PER_STEP // 2
    width = n_pairs * LANES
    assert t % tq == 0 and tq % tk == 0 and 2 * hd == LANES and d % width == 0
    assert SB_BLOCKS_PER_VISIT % (tq // tk) == 0 and SB_BLOCKS_PER_VISIT <= 2 * (tq // tk)
    nq = t // tq
    kern = functools.partial(_sb_prompt_kernel, tq=tq, tk=tk, hd=hd, n_pairs=n_pairs)
    return pl.pallas_call(
        kern, grid=(batch, d // width, nq),
        in_specs=[pl.BlockSpec(memory_space=pltpu.SMEM),
                  pl.BlockSpec((tq, width), lambda b, g, i: (b * nq + i, g)),
                  pl.BlockSpec((t, width), lambda b, g, i: (b, g)),
                  pl.BlockSpec((t, width), lambda b, g, i: (b, g))],
        out_specs=pl.BlockSpec((tq, width), lambda b, g, i: (b * nq + i, g)),
        out_shape=jax.ShapeDtypeStruct((n_p, d), BF16),
        scratch_shapes=[pltpu.VMEM((2 * n_pairs, tq, LANES), BF16), pltpu.VMEM((tk, tk), BF16),
                        pltpu.VMEM((2 * n_pairs, tq, tk), F32), pltpu.VMEM((tq, width), F32),
                        pltpu.VMEM((n_pairs, tq, LANES), F32)],
        compiler_params=_params("arbitrary", "arbitrary", "arbitrary"), name="sb_prompt",
    )(bias, q, k, v)


def _sb_sample_kernel(pt_ref, q_ref, kn_ref, vn_ref, bias_ref, *refs, t_dec, hd, n_pages):
    del pt_ref
    kt_refs, vt_refs, o_ref = refs[:n_pages], refs[n_pages:2 * n_pages], refs[2 * n_pages]
    page = kt_refs[0].shape[2]
    nrow = t_dec * SB_HEADS
    d = SB_HEADS * hd
    blk = 2 * page
    head_mask = ((lax.broadcasted_iota(jnp.int32, (SB_HEADS, d), 1) // hd)
                 == lax.broadcasted_iota(jnp.int32, (SB_HEADS, d), 0))
    bias = bias_ref[:, :1]
    qv = q_ref[0]
    qbd = jnp.concatenate([jnp.where(head_mask, qv[t:t + 1, :], 0.0) for t in range(t_dec)]
                          + [jnp.zeros((LANES - nrow, d), F32)], axis=0).astype(BF16)
    key_i = lax.broadcasted_iota(jnp.int32, (LANES, blk), 1)
    qry_t = lax.broadcasted_iota(jnp.int32, (LANES, blk), 0) // SB_HEADS
    y_new = _dot_nt(qbd, _pad_rows(kn_ref[0], blk).astype(BF16)) + bias
    y_pages = [_dot(qbd, kt[0].astype(BF16)) for kt in kt_refs]
    ys = [jnp.where(key_i < qry_t, y_new, MASKED_LOGIT)]
    ys += [jnp.concatenate([y_pages[2 * i], y_pages[2 * i + 1]], axis=1) + bias for i in reversed(range(n_pages // 2))]
    blocks = _sb_blocks(ys, _suffix_ones(blk))
    carry = jnp.zeros((LANES, 1), F32)
    weights = []
    for a, tot in blocks:
        weights.append((a * jnp.exp2(-carry)).astype(BF16))
        carry = carry + tot
    out_new = _dot(weights[0], _pad_rows(vn_ref[0], blk).astype(BF16))
    out_t = None
    for w, i in zip(weights[1:], reversed(range(n_pages // 2))):
        v_t = jnp.concatenate([vt_refs[2 * i][0], vt_refs[2 * i + 1][0]], axis=1).astype(BF16)
        part = _dot_nt(v_t, w)
        out_t = part if out_t is None else out_t + part
    rows = [jnp.sum(jnp.where(head_mask, out_new[t * SB_HEADS:(t + 1) * SB_HEADS, :], 0.0), axis=0, keepdims=True)
            for t in range(t_dec)]
    f_head = lax.broadcasted_iota(jnp.int32, (d, LANES), 0) // hd
    r_idx = lax.broadcasted_iota(jnp.int32, (d, LANES), 1)
    hi, lo = _split_bf16(jnp.where((r_idx % SB_HEADS == f_head) & (r_idx < nrow), out_t, 0.0))
    sel = jnp.where(lax.broadcasted_iota(jnp.int32, (16, LANES), 1) // SB_HEADS
                    == lax.broadcasted_iota(jnp.int32, (16, LANES), 0), 1.0, 0.0).astype(BF16)
    past = _dot_nt(sel, hi) + _dot_nt(sel, lo)
    o_ref[0] = jnp.concatenate(rows, axis=0) + past[:t_dec]


def _feature_major_pages(cache):
    n_pool, page, heads, hd = cache.shape
    return jnp.transpose(cache, (0, 2, 3, 1)).reshape(n_pool, heads * hd, page)


def _sb_sample(q, k_new, v_new, pages_kt, pages_vt, page_table, bias, t_dec, hd):
    n_s, d = q.shape
    n_seq = n_s // t_dec
    page = pages_kt.shape[2]
    n_pages = page_table.shape[1]
    nrow = t_dec * SB_HEADS
    assert page == LANES and nrow <= LANES and n_pages % 2 == 0 and t_dec <= 16
    bias_rows = jnp.broadcast_to(jnp.tile(bias.astype(F32), t_dec)[:, None], (nrow, LANES))
    bias_rows = jnp.concatenate([bias_rows, jnp.zeros((LANES - nrow, LANES), F32)], axis=0)
    seq3 = lambda a: a.reshape(n_seq, t_dec, d)
    seq_spec = pl.BlockSpec((1, t_dec, d), lambda b, pt: (b, 0, 0))

    def page_spec(j):
        return pl.BlockSpec((1, d, page), lambda b, pt: (pt[b * n_pages + j], 0, 0))

    kern = functools.partial(_sb_sample_kernel, t_dec=t_dec, hd=hd, n_pages=n_pages)
    out = pl.pallas_call(
        kern,
        grid_spec=pltpu.PrefetchScalarGridSpec(
            num_scalar_prefetch=1, grid=(n_seq,),
            in_specs=[seq_spec, seq_spec, seq_spec, pl.BlockSpec((LANES, LANES), lambda b, pt: (0, 0))]
                     + [page_spec(j) for j in range(n_pages)] + [page_spec(j) for j in range(n_pages)],
            out_specs=seq_spec),
        out_shape=jax.ShapeDtypeStruct((n_seq, t_dec, d), F32),
        compiler_params=_params("arbitrary"), name="sb_sample",
    )(page_table.reshape(-1), seq3(q), seq3(k_new), seq3(v_new), bias_rows,
      *([pages_kt] * n_pages), *([pages_vt] * n_pages))
    return out.reshape(n_s, d)


def kernel(x_prompt, x_sample, p_prompt, p_sample, state_ret, cache_k, cache_v, page_table, g_norm, w_ff1_gate, w_ff1_up, w_ff1_down, w_ff2_gate, w_ff2_up, w_ff2_down, w_ple_up, w_ple_gate, g_ple, w_ret_in, w_ret_out, g_ret_gn, g_kv, w_kv, w_sb_q, w_sb_out, b_sb, g_final):
    batch, t_prompt, d = x_prompt.shape
    n_dec, t_dec, _ = x_sample.shape
    depth = g_norm.shape[0]
    n_ret = w_ret_in.shape[0]
    dk = d // RET_HEADS
    dv = w_ret_out.shape[1] // RET_HEADS
    hd = d // SB_HEADS
    past_len = page_table.shape[1] * cache_k.shape[1]
    row = lambda a: a.reshape(1, -1)
    bf = lambda a: a.astype(BF16)

    xp = x_prompt.reshape(batch * t_prompt, d)
    xs = x_sample.reshape(n_dec * t_dec, d)
    x_out = [(d, F32)]
    states_p, states_s = [], None
    pages_kt, pages_vt = _feature_major_pages(cache_k), _feature_major_pages(cache_v)
    ff1 = [bf(w_ff1_gate), bf(w_ff1_up), bf(w_ff1_down)]
    ff2 = [bf(w_ff2_gate), bf(w_ff2_up), bf(w_ff2_down)]
    ple = [bf(w_ple_up), bf(w_ple_gate)]
    ret_in, ret_out, sb_q, sb_out = bf(w_ret_in), bf(w_ret_out), bf(w_sb_q), bf(w_sb_out)
    p_p = p_prompt.reshape(depth, batch * t_prompt, -1)
    p_s = p_sample.reshape(depth, n_dec * t_dec, -1)
    for i in range(depth):
        ffn1_consts = [row(g_norm[i, 0])] + [_Layer(w, i) for w in ff1]
        streams = [(xp, xs), (_Layer(p_p, i), _Layer(p_s, i))]
        if i < n_ret:
            (xp, xs), = _tokenwise(_ffn_body, [(xp, xs)], ffn1_consts, x_out, "ffn1")
            g_gn = row(g_ret_gn[i])
            xp, s_p = _retention_prompt(xp, batch, row(g_norm[i, 1]), _Layer(ret_in, i), _Layer(ret_out, i),
                                        g_gn, dk, dv)
            q, k, v, g = _rowwise(functools.partial(_ret_in_proj_body, qk=RET_HEADS * dk, vd=RET_HEADS * dv),
                                  [xs], [row(g_norm[i, 1]), _Layer(ret_in, i)],
                                  [(RET_HEADS * dk, F32), (RET_HEADS * dk, F32),
                                   (RET_HEADS * dv, F32), (RET_HEADS * dv, F32)], "ret_in_proj_sample")
            gated, states_s = _retention_sample(q, k, v, g, g_gn, state_ret, i, states_s, t_dec, past_len, dk, dv)
            xs, = _rowwise(_proj_residual_body, [xs, gated], [_Layer(ret_out, i)], x_out, "ret_out_proj_sample")
            states_p.append(s_p)
            streams[0] = (xp, xs)
            mixer_out = []
        else:
            l = i - n_ret
            (xp, xs), (qp, qs) = _tokenwise(functools.partial(_ffn_q_body, scale=LOG2E * hd ** -0.5), [(xp, xs)],
                                            ffn1_consts + [row(g_norm[i, 1]), _Layer(sb_q, l)],
                                            x_out + [(d, BF16)], "ffn1_q")
            bias2 = b_sb[l].astype(F32) * LOG2E
            o_p = _sb_prompt(qp, kp16, vp16, bias2, batch, hd)
            o_s = _sb_sample(qs.astype(F32), ks32, vs32, pages_kt, pages_vt, page_table, bias2, t_dec, hd)
            streams = [(xp, xs), streams[1], (o_p, bf(o_s))]
            mixer_out = [_Layer(sb_out, l)]
        last = i == depth - 1
        consts = ([row(g_norm[i, 2])] + [_Layer(w, i) for w in ff2] + [row(g_ple[i])] + [_Layer(w, i) for w in ple]
                  + mixer_out + ([row(g_final)] if last else []))
        (xp, xs), = _tokenwise(functools.partial(_ffn_ple_body, mixer_proj=bool(mixer_out), final=last),
                               streams, consts, x_out, "ffn2_ple")
        if i == n_ret - 1:
            kpt, vpt, kp16, vp16 = _kv_prompt(xp, batch, row(g_kv), bf(w_kv))
            ks32, vs32 = _rowwise(_kv_body, [xs], [row(g_kv), bf(w_kv)], [(d, F32), (d, F32)], "kv_proj_sample")
    yp, ys = xp, xs
    heads_last = lambda a: jnp.transpose(a.reshape(batch, SB_HEADS, hd, t_prompt), (0, 3, 1, 2))
    return (yp.reshape(batch, t_prompt, d), ys.reshape(n_dec, t_dec, d),
            jnp.stack(states_p), states_s,
            heads_last(kpt), heads_last(vpt),
            ks32.reshape(n_dec, t_dec, SB_HEADS, hd), vs32.reshape(n_dec, t_dec, SB_HEADS, hd))
```

```python
import functools
import math

import jax
import jax.numpy as jnp
from jax import lax
from jax.experimental import pallas as pl
from jax.experimental.pallas import tpu as pltpu

F32 = jnp.float32
BF16 = jnp.bfloat16

EPS = 1e-6
ROPE_BASE = 10000.0
RET_HEADS = 4
SB_HEADS = 16
LANES = 128
VMEM_LIMIT_BYTES = 58 * 1024 * 1024
MASKED_LOGIT = -1e30
LOG2E = 1.4426950408889634
TOKEN_TILE = 512
RET_CHUNK_ROWS = 512
SB_QUERY_BLOCK = 512
SB_KEY_BLOCK = 256
SB_HEADS_PER_STEP = 4
SB_BLOCKS_PER_VISIT = 4


def _params(*sem, flags=None):
    return pltpu.CompilerParams(dimension_semantics=sem, vmem_limit_bytes=VMEM_LIMIT_BYTES, flags=flags)


def _dot(a, b):
    return jnp.dot(a, b, preferred_element_type=F32)


def _dot_nt(a, b):
    return lax.dot_general(a, b, (((1,), (1,)), ((), ())), preferred_element_type=F32)


def _rmsnorm(x, g):
    y = x * lax.rsqrt(jnp.mean(x * x, axis=-1, keepdims=True) + EPS)
    return y * g


def _silu(x):
    return x * jax.nn.sigmoid(x)


def _softplus2(y):
    neg_abs = lax.bitcast_convert_type(lax.bitcast_convert_type(y, jnp.int32) | jnp.int32(-2 ** 31), F32)
    return jnp.maximum(y, 0.0) + jnp.log(1.0 + jnp.exp2(neg_abs)) * LOG2E


def _split_bf16(x):
    hi = x.astype(BF16)
    lo = (x - hi.astype(F32)).astype(BF16)
    return hi, lo


class _Layer:
    def __init__(self, stack, layer):
        self.stack, self.layer = stack, layer
        self.shape = stack.shape[1:]


def _array(a):
    return a.stack if isinstance(a, _Layer) else a


def _resident(arr):
    nd = len(arr.shape)
    if isinstance(arr, _Layer):
        return pl.BlockSpec((None,) + tuple(arr.shape), lambda *_: (arr.layer,) + (0,) * nd,
                            pipeline_mode=pl.Buffered(1))
    return pl.BlockSpec(arr.shape, lambda *_: (0,) * nd, pipeline_mode=pl.Buffered(1))


def _tokenwise(body, streams, consts, outs, name):
    n_p, n_s = streams[0][0].shape[0], streams[0][1].shape[0]
    tm = min(TOKEN_TILE, n_s)
    assert n_p % tm == 0 and n_s % tm == 0
    p_tiles, s_tiles = n_p // tm, n_s // tm
    ns, nc, no = len(streams), len(consts), len(outs)

    def kern(*refs):
        in_refs, c_refs, o_refs = refs[:2 * ns], refs[2 * ns:2 * ns + nc], refs[2 * ns + nc:]
        is_prompt = pl.program_id(0) < p_tiles
        vals = [jnp.where(is_prompt, in_refs[2 * k][...], in_refs[2 * k + 1][...]) for k in range(ns)]
        res = body(*vals, *c_refs)

        @pl.when(is_prompt)
        def _():
            for k in range(no):
                o_refs[2 * k][...] = res[k].astype(outs[k][1])

        @pl.when(jnp.logical_not(is_prompt))
        def _():
            for k in range(no):
                o_refs[2 * k + 1][...] = res[k].astype(outs[k][1])

    def p_spec(w, layer=None):
        if layer is not None:
            return pl.BlockSpec((None, tm, w), lambda i: (layer, jnp.minimum(i, p_tiles - 1), 0))
        return pl.BlockSpec((tm, w), lambda i: (jnp.minimum(i, p_tiles - 1), 0))

    def s_spec(w, layer=None):
        if layer is not None:
            return pl.BlockSpec((None, tm, w), lambda i: (layer, jnp.maximum(i - p_tiles, 0), 0))
        return pl.BlockSpec((tm, w), lambda i: (jnp.maximum(i - p_tiles, 0), 0))

    in_specs, args = [], []
    for a_p, a_s in streams:
        layer = a_p.layer if isinstance(a_p, _Layer) else None
        in_specs += [p_spec(a_p.shape[1], layer), s_spec(a_s.shape[1], layer)]
        args += [_array(a_p), _array(a_s)]
    in_specs += [_resident(c) for c in consts]
    consts = [_array(c) for c in consts]
    out_specs, out_shape = [], []
    for w, dt in outs:
        out_specs += [p_spec(w), s_spec(w)]
        out_shape += [jax.ShapeDtypeStruct((n_p, w), dt), jax.ShapeDtypeStruct((n_s, w), dt)]
    res = pl.pallas_call(
        kern, grid=(p_tiles + s_tiles,), in_specs=in_specs, out_specs=out_specs,
        out_shape=out_shape, compiler_params=_params("arbitrary"), name=name,
    )(*args, *consts)
    return [(res[2 * k], res[2 * k + 1]) for k in range(no)]


def _rowwise(body, rows, consts, outs, name):
    n = rows[0].shape[0]
    tm = min(TOKEN_TILE, n)
    assert n % tm == 0
    nr, nc, no = len(rows), len(consts), len(outs)

    def kern(*refs):
        res = body(*[r[...] for r in refs[:nr]], *refs[nr:nr + nc])
        for k in range(no):
            refs[nr + nc + k][...] = res[k].astype(outs[k][1])

    in_specs = [pl.BlockSpec((tm, a.shape[1]), lambda i: (i, 0)) for a in rows]
    in_specs += [_resident(c) for c in consts]
    consts = [_array(c) for c in consts]
    return pl.pallas_call(
        kern, grid=(n // tm,), in_specs=in_specs,
        out_specs=[pl.BlockSpec((tm, w), lambda i: (i, 0)) for w, _ in outs],
        out_shape=[jax.ShapeDtypeStruct((n, w), dt) for w, dt in outs],
        compiler_params=_params("arbitrary"), name=name,
    )(*rows, *consts)


def _ffn_body(x, g_ref, wg_ref, wu_ref, wd_ref):
    h = _rmsnorm(x, g_ref[...]).astype(BF16)
    d_ff = wg_ref.shape[1]
    chunk = 256 if d_ff % 256 == 0 else LANES
    acc = None
    for c0 in range(0, d_ff, chunk):
        gate = _dot(h, wg_ref[:, c0:c0 + chunk])
        up = _dot(h, wu_ref[:, c0:c0 + chunk])
        d = _dot((_silu(gate) * up).astype(BF16), wd_ref[c0:c0 + chunk, :])
        acc = d if acc is None else acc + d
    return (x + 0.5 * acc,)


def _ple_body(x, p, g_ref, wup_ref, wgate_ref):
    h = _rmsnorm(x, g_ref[...]).astype(BF16)
    gate = jax.nn.sigmoid(_dot(h, wgate_ref[...]))
    return (x + _dot(p.astype(BF16), wup_ref[...]) * gate,)


def _ffn_q_body(x, g_ref, wg_ref, wu_ref, wd_ref, gq_ref, wq_ref, *, scale):
    (x,) = _ffn_body(x, g_ref, wg_ref, wu_ref, wd_ref)
    return (x, _dot(_rmsnorm(x, gq_ref[...]).astype(BF16), wq_ref[...]) * scale)


def _kv_body(x, g_ref, w_ref):
    h = _rmsnorm(x, g_ref[...]).astype(BF16)
    half = w_ref.shape[1] // 2
    return (_dot(h, w_ref[:, :half]), _dot(h, w_ref[:, half:]))


def _kv_prompt_kernel(x_ref, g_ref, w_ref, kt_ref, vt_ref, k16_ref, v16_ref):
    k, v = _kv_body(x_ref[...], g_ref, w_ref)
    kt_ref[0] = k.T
    vt_ref[0] = v.T
    k16_ref[...] = k.astype(BF16)
    v16_ref[...] = v.astype(BF16)


def _kv_prompt(xp, batch, g, w):
    n_p, d = xp.shape
    t = n_p // batch
    tm = min(TOKEN_TILE, t)
    assert t % tm == 0
    nt = t // tm
    row_spec = pl.BlockSpec((tm, d), lambda b, i: (b * nt + i, 0))
    fm_spec = pl.BlockSpec((1, d, tm), lambda b, i: (b, 0, i))
    return pl.pallas_call(
        _kv_prompt_kernel, grid=(batch, nt),
        in_specs=[row_spec, _resident(g), _resident(w)],
        out_specs=[fm_spec, fm_spec, row_spec, row_spec],
        out_shape=[jax.ShapeDtypeStruct((batch, d, t), F32), jax.ShapeDtypeStruct((batch, d, t), F32),
                   jax.ShapeDtypeStruct((n_p, d), BF16), jax.ShapeDtypeStruct((n_p, d), BF16)],
        compiler_params=_params("arbitrary", "arbitrary"), name="kv_proj_prompt",
    )(xp, g, w)


def _proj_residual_body(x, a, w_ref):
    return (x + _dot(a.astype(BF16), w_ref[...]),)


def _ffn_ple_body(x, p, *rest, mixer_proj, final):
    rest = list(rest)
    mixed = rest.pop(0) if mixer_proj else None
    ffn_refs, ple_refs, tail = rest[:4], rest[4:7], rest[7:]
    if mixer_proj:
        x = x + _dot(mixed, tail.pop(0)[...])
    (x,) = _ffn_body(x, *ffn_refs)
    (x,) = _ple_body(x, p, *ple_refs)
    return (_rmsnorm(x, tail[0][...]) if final else x,)


def _ret_in_proj_body(x, g_ref, w_ref, *, qk, vd):
    h = _rmsnorm(x, g_ref[...]).astype(BF16)
    return (_dot(h, w_ref[:, :qk]), _dot(h, w_ref[:, qk:2 * qk]),
            _dot(h, w_ref[:, 2 * qk:2 * qk + vd]), _dot(h, w_ref[:, 2 * qk + vd:]))


def _rope_tables(pos, half):
    inv_freq = ROPE_BASE ** (-jnp.arange(half, dtype=F32) / half)
    ang = pos.astype(F32)[:, None] * inv_freq[None, :]
    return jnp.cos(ang), jnp.sin(ang)


def _rope(x, cos, sin):
    half = x.shape[-1] // 2
    x1, x2 = x[:, :half], x[:, half:]
    return jnp.concatenate([x1 * cos - x2 * sin, x2 * cos + x1 * sin], axis=-1)


def _log_decay(head):
    return math.log(1.0 - 2.0 ** (-5.0 - head))


def _group_norm_gate(o, g, ggn):
    mu = jnp.mean(o, axis=-1, keepdims=True)
    var = jnp.mean(jnp.square(o - mu), axis=-1, keepdims=True)
    on = (o - mu) * lax.rsqrt(var + EPS) * ggn
    return _silu(g) * on


def _ret_prompt_kernel(x_ref, gn_ref, win_ref, wout_ref, ggn_ref, cos_ref, sin_ref,
                       o_ref, s_ref, *, rows, dk, dv):
    qk, vd = RET_HEADS * dk, RET_HEADS * dv

    @pl.when(pl.program_id(1) == 0)
    def _():
        s_ref[...] = jnp.zeros_like(s_ref)

    x = x_ref[...]
    h = _rmsnorm(x, gn_ref[...]).astype(BF16)
    cos, sin = cos_ref[...], sin_ref[...]
    diff = (lax.broadcasted_iota(jnp.int32, (rows, rows), 0)
            - lax.broadcasted_iota(jnp.int32, (rows, rows), 1)).astype(F32)
    ri = lax.broadcasted_iota(jnp.int32, (rows, 1), 0).astype(F32)
    acc = None
    for hh in range(RET_HEADS):
        lg = _log_decay(hh)
        q = _rope(_dot(h, win_ref[:, hh * dk:(hh + 1) * dk]), cos, sin)
        k = _rope(_dot(h, win_ref[:, qk + hh * dk:qk + (hh + 1) * dk]), cos, sin) * (dk ** -0.5)
        v = _dot(h, win_ref[:, 2 * qk + hh * dv:2 * qk + (hh + 1) * dv]).astype(BF16)
        g = _dot(h, win_ref[:, 2 * qk + vd + hh * dv:2 * qk + vd + (hh + 1) * dv])
        decay = jnp.where(diff >= 0, jnp.exp(jnp.maximum(diff, 0.0) * lg), 0.0)
        s = _dot_nt(q.astype(BF16), k.astype(BF16)) * decay
        state = s_ref[0, hh]
        q_dec = (q * jnp.exp((ri + 1.0) * lg)).astype(BF16)
        o = _dot(s.astype(BF16), v) + _dot(q_dec, state.astype(BF16))
        k_dec = k * jnp.exp((rows - 1.0 - ri) * lg)
        s_ref[0, hh] = math.exp(rows * lg) * state + _dot(k_dec.T.astype(BF16), v)
        gated = _group_norm_gate(o, g, ggn_ref[:, hh * dv:(hh + 1) * dv]).astype(BF16)
        d = _dot(gated, wout_ref[hh * dv:(hh + 1) * dv, :])
        acc = d if acc is None else acc + d
    o_ref[...] = x + acc


def _retention_prompt(xp, batch, g, w_in, w_out, g_gn, dk, dv):
    n_p, d = xp.shape
    t = n_p // batch
    rows = min(RET_CHUNK_ROWS, t)
    assert t % rows == 0
    nch = t // rows
    cos, sin = _rope_tables(jnp.arange(t, dtype=jnp.int32), dk // 2)
    kern = functools.partial(_ret_prompt_kernel, rows=rows, dk=dk, dv=dv)
    return pl.pallas_call(
        kern, grid=(batch, nch),
        in_specs=[pl.BlockSpec((rows, d), lambda b, c: (b * nch + c, 0)),
                  _resident(g), _resident(w_in), _resident(w_out), _resident(g_gn),
                  pl.BlockSpec((rows, dk // 2), lambda b, c: (c, 0)),
                  pl.BlockSpec((rows, dk // 2), lambda b, c: (c, 0))],
        out_specs=[pl.BlockSpec((rows, d), lambda b, c: (b * nch + c, 0)),
                   pl.BlockSpec((1, RET_HEADS, dk, dv), lambda b, c: (b, 0, 0, 0))],
        out_shape=[jax.ShapeDtypeStruct((n_p, d), F32),
                   jax.ShapeDtypeStruct((batch, RET_HEADS, dk, dv), F32)],
        compiler_params=_params("arbitrary", "arbitrary"), name="retention_prompt",
    )(xp, g, _array(w_in), _array(w_out), g_gn, cos, sin)


def _pad_rows(x, rows):
    return jnp.concatenate([x, jnp.zeros((rows - x.shape[0], x.shape[1]), x.dtype)], axis=0)


def _ret_sample_kernel(q_ref, k_ref, v_ref, g_ref, ggn_ref, cos_ref, sin_ref, s_ref, *rest,
                       t_dec, dk, dv, n_fill):
    o_ref, so_ref = rest[-2], rest[-1]
    nrow = 2 * t_dec
    cos, sin = cos_ref[...], sin_ref[...]
    r_i = lax.broadcasted_iota(jnp.int32, (nrow, LANES), 0)
    c_i = lax.broadcasted_iota(jnp.int32, (nrow, LANES), 1)
    same_seq = (r_i // t_dec) == (c_i // t_dec)
    diff = ((r_i % t_dec) - (c_i % t_dec)).astype(F32)
    valid = same_seq & (diff >= 0) & (c_i < nrow)
    t_row = (lax.broadcasted_iota(jnp.int32, (nrow, 1), 0) % t_dec).astype(F32)
    seq_row = lax.broadcasted_iota(jnp.int32, (nrow, 1), 0) // t_dec
    seq_lane = lax.broadcasted_iota(jnp.int32, (1, LANES), 1) // t_dec
    for fill in range(n_fill):
        so_ref[1 + fill] = jnp.zeros(so_ref.shape[1:], so_ref.dtype)
    outs = []
    for hh in range(RET_HEADS):
        lg = _log_decay(hh)
        q = _rope(q_ref[0, :, hh * dk:(hh + 1) * dk], cos, sin)
        k = _rope(k_ref[0, :, hh * dk:(hh + 1) * dk], cos, sin) * (dk ** -0.5)
        v = _pad_rows(v_ref[0, :, hh * dv:(hh + 1) * dv], LANES).astype(BF16)
        g = g_ref[0, :, hh * dv:(hh + 1) * dv]
        decay = jnp.where(valid, jnp.exp(jnp.maximum(diff, 0.0) * lg), 0.0)
        q16 = _pad_rows(q, 16)
        s = _dot_nt(q16.astype(BF16), _pad_rows(k, LANES).astype(BF16))[:nrow] * decay
        o = _dot(_pad_rows(s, 16).astype(BF16), v)[:nrow]
        q_dec = _pad_rows(q * jnp.exp((t_row + 1.0) * lg), 16).astype(BF16)
        k_dec_t = _pad_rows(k * jnp.exp((t_dec - 1.0 - t_row) * lg), LANES).T
        for sq in range(2):
            state = s_ref[0, sq, hh]
            o_state = _dot(q_dec, state.astype(BF16))[:nrow]
            o = o + jnp.where(seq_row == sq, o_state, 0.0)
            k_sq = jnp.where(seq_lane == sq, k_dec_t, 0.0).astype(BF16)
            so_ref[0, sq, hh] = math.exp(t_dec * lg) * state + _dot(k_sq, v)
        outs.append(_group_norm_gate(o, g, ggn_ref[:, hh * dv:(hh + 1) * dv]))
    o_ref[0] = jnp.concatenate(outs, axis=-1)


def _retention_sample(q, k, v, g, g_gn, state_all, layer, prev_states, t_dec, past_len, dk, dv):
    n_s = q.shape[0]
    n_seq = n_s // t_dec
    n_layers = state_all.shape[0]
    assert n_seq % 2 == 0
    nrow = 2 * t_dec
    pos = past_len + (jnp.arange(nrow, dtype=jnp.int32) % t_dec)
    cos, sin = _rope_tables(pos, dk // 2)
    grp = lambda a: a.reshape(n_seq // 2, nrow, a.shape[1])
    qk, vd = RET_HEADS * dk, RET_HEADS * dv
    first = prev_states is None
    n_fill = n_layers - 1 if first else 0
    kern = functools.partial(_ret_sample_kernel, t_dec=t_dec, dk=dk, dv=dv, n_fill=n_fill)
    row_spec = lambda w: pl.BlockSpec((1, nrow, w), lambda i: (i, 0, 0))
    st_in = pl.BlockSpec((1, 2, RET_HEADS, dk, dv), lambda i: (layer, i, 0, 0, 0))
    st_out = pl.BlockSpec((1 + n_fill, 2, RET_HEADS, dk, dv), lambda i: (0 if first else layer, i, 0, 0, 0))
    in_specs = [row_spec(qk), row_spec(qk), row_spec(vd), row_spec(vd), _resident(g_gn),
                _resident(cos), _resident(sin), st_in]
    args = [grp(q), grp(k), grp(v), grp(g), g_gn, cos, sin, state_all]
    aliases = {}
    if not first:
        in_specs.append(pl.BlockSpec(memory_space=pl.ANY))
        args.append(prev_states)
        aliases = {len(args) - 1: 1}
    gated, s_new = pl.pallas_call(
        kern, grid=(n_seq // 2,), in_specs=in_specs,
        out_specs=[row_spec(vd), st_out],
        out_shape=[jax.ShapeDtypeStruct((n_seq // 2, nrow, vd), F32),
                   jax.ShapeDtypeStruct(state_all.shape, F32)],
        input_output_aliases=aliases,
        compiler_params=_params("arbitrary"), name="retention_sample",
    )(*args)
    return gated.reshape(n_s, vd), s_new


def _suffix_ones(n, strict=False):
    r = lax.broadcasted_iota(jnp.int32, (n, n), 0)
    c = lax.broadcasted_iota(jnp.int32, (n, n), 1)
    return jnp.where((r > c) if strict else (r >= c), 1.0, 0.0).astype(BF16)


def _sb_blocks(ys, ones):
    parts = [_split_bf16(_softplus2(y)) for y in ys]
    suffixes = [_dot(hi, ones) + _dot(lo, ones) for hi, lo in parts]
    return [(jnp.exp2(y - sfx), sfx[:, :1]) for y, sfx in zip(ys, suffixes)]


def _sb_blocks_skewed(n, logits, strict_ones, use):
    def own_and_later(y):
        sp = _softplus2(y)
        return y - sp, _dot(sp.astype(BF16), strict_ones), sp[:, :1]

    parts, results = {}, []
    for step in range(n + 2):
        if step < n:
            parts[step] = logits(step)
        if 1 <= step <= n:
            parts[step - 1] = own_and_later(parts[step - 1])
        if step >= 2:
            own, later, first = parts.pop(step - 2)
            results.append(use(step - 2, jnp.exp2(own - later), later[:, :1] + first))
    return results


def _sb_prompt_kernel(bias_ref, q_ref, k_ref, v_ref, o_ref, qh_ref, ones_ref, bm_ref, acc_ref, carry_ref,
                      *, tq, tk, hd, n_pairs):
    grp, qi = pl.program_id(1), pl.program_id(2)
    ratio = tq // tk
    lo_lanes = lax.broadcasted_iota(jnp.int32, (1, LANES), 1) < hd
    row = lax.broadcasted_iota(jnp.int32, (tq, tk), 0)
    col = lax.broadcasted_iota(jnp.int32, (tq, tk), 1)
    for p in range(n_pairs):
        qp = q_ref[:, p * LANES:(p + 1) * LANES]
        zero = jnp.zeros_like(qp)
        qh_ref[2 * p] = jnp.where(lo_lanes, qp, zero)
        qh_ref[2 * p + 1] = jnp.where(lo_lanes, zero, qp)

    @pl.when(qi == 0)
    def _():
        for h in range(2 * n_pairs):
            bm_ref[h] = jnp.where(col < row, bias_ref[grp * 2 * n_pairs + h], MASKED_LOGIT)
        ones_ref[...] = _suffix_ones(tk, strict=True)

    acc_ref[...] = jnp.zeros_like(acc_ref)
    carry_ref[...] = jnp.zeros_like(carry_ref)
    n_heads = 2 * n_pairs
    past = ratio * qi

    def visit(top, count, overlap):
        ones = ones_ref[...]
        starts = [pl.multiple_of((top - u) * tk, tk) for u in range(count)]
        row0 = [(ratio - 1 - u) * tk if overlap else 0 for u in range(count)]
        biases = [bias_ref[grp * n_heads + h] for h in range(n_heads)]

        def logits(i):
            u, h = divmod(i, n_heads)
            kb = k_ref[pl.ds(starts[u], tk), (h // 2) * LANES:(h // 2 + 1) * LANES]
            s = _dot_nt(qh_ref[h, row0[u]:, :], kb)
            return s + (bm_ref[h, :tq - row0[u], :] if overlap else biases[h])

        def weigh(i, a, tot):
            u, h = divmod(i, n_heads)
            vb = v_ref[pl.ds(starts[u], tk), (h // 2) * LANES:(h // 2 + 1) * LANES]
            vzero = jnp.zeros_like(vb)
            vh = jnp.where(lo_lanes, vb, vzero) if h % 2 == 0 else jnp.where(lo_lanes, vzero, vb)
            return _dot(a.astype(BF16), vh), tot

        blocks = _sb_blocks_skewed(count * n_heads, logits, ones, weigh)
        for p in range(n_pairs):
            lanes = slice(p * LANES, (p + 1) * LANES)
            if overlap:
                for u in range(count):
                    (d0, t0), (d1, t1) = blocks[u * n_heads + 2 * p], blocks[u * n_heads + 2 * p + 1]
                    c = carry_ref[p, row0[u]:, :]
                    acc_ref[row0[u]:, lanes] += jnp.exp2(-c) * (d0 + d1)
                    carry_ref[p, row0[u]:, :] = c + jnp.where(lo_lanes, t0, t1)
            else:
                c = carry_ref[p]
                acc = acc_ref[:, lanes]
                for u in range(count):
                    (d0, t0), (d1, t1) = blocks[u * n_heads + 2 * p], blocks[u * n_heads + 2 * p + 1]
                    acc = acc + jnp.exp2(-c) * (d0 + d1)
                    c = c + jnp.where(lo_lanes, t0, t1)
                acc_ref[:, lanes] = acc
                carry_ref[p] = c

    visit(past + ratio - 1, ratio, True)
    big = SB_BLOCKS_PER_VISIT
    rem = lax.rem(past, big)
    if big > ratio:
        @pl.when(rem != 0)
        def _():
            visit(past - 1, ratio, False)

    def body(it, _):
        visit(past - 1 - rem - it * big, big, False)
        return 0

    lax.fori_loop(0, past // big, body, 0)
    o_ref[...] = acc_ref[...].astype(o_ref.dtype)


def _sb_prompt(q, k, v, bias, batch, hd):
    n_p, d = q.shape
    t = n_p // batch
    tq, tk = min(SB_QUERY_BLOCK, t), min(SB_KEY_BLOCK, t)
    n_pairs = SB_HEADS_PER_STEP // 2
    width = n_pairs * LANES
    assert t % tq == 0 and tq % tk == 0 and 2 * hd == LANES and d % width == 0
    assert SB_BLOCKS_PER_VISIT % (tq // tk) == 0 and SB_BLOCKS_PER_VISIT <= 2 * (tq // tk)
    nq = t // tq
    kern = functools.partial(_sb_prompt_kernel, tq=tq, tk=tk, hd=hd, n_pairs=n_pairs)
    return pl.pallas_call(
        kern, grid=(batch, d // width, nq),
        in_specs=[pl.BlockSpec(memory_space=pltpu.SMEM),
                  pl.BlockSpec((tq, width), lambda b, g, i: (b * nq + i, g)),
                  pl.BlockSpec((t, width), lambda b, g, i: (b, g)),
                  pl.BlockSpec((t, width), lambda b, g, i: (b, g))],
        out_specs=pl.BlockSpec((tq, width), lambda b, g, i: (b * nq + i, g)),
        out_shape=jax.ShapeDtypeStruct((n_p, d), BF16),
        scratch_shapes=[pltpu.VMEM((2 * n_pairs, tq, LANES), BF16), pltpu.VMEM((tk, tk), BF16),
                        pltpu.VMEM((2 * n_pairs, tq, tk), F32), pltpu.VMEM((tq, width), F32),
                        pltpu.VMEM((n_pairs, tq, LANES), F32)],
        compiler_params=_params("arbitrary", "arbitrary", "arbitrary"), name="sb_prompt",
    )(bias, q, k, v)


def _sb_sample_kernel(pt_ref, q_ref, kn_ref, vn_ref, bias_ref, *refs, t_dec, hd, n_pages):
    del pt_ref
    kt_refs, vt_refs, o_ref = refs[:n_pages], refs[n_pages:2 * n_pages], refs[2 * n_pages]
    page = kt_refs[0].shape[2]
    nrow = t_dec * SB_HEADS
    d = SB_HEADS * hd
    blk = 2 * page
    head_mask = ((lax.broadcasted_iota(jnp.int32, (SB_HEADS, d), 1) // hd)
                 == lax.broadcasted_iota(jnp.int32, (SB_HEADS, d), 0))
    bias = bias_ref[:, :1]
    qv = q_ref[0]
    qbd = jnp.concatenate([jnp.where(head_mask, qv[t:t + 1, :], 0.0) for t in range(t_dec)]
                          + [jnp.zeros((LANES - nrow, d), F32)], axis=0).astype(BF16)
    key_i = lax.broadcasted_iota(jnp.int32, (LANES, blk), 1)
    qry_t = lax.broadcasted_iota(jnp.int32, (LANES, blk), 0) // SB_HEADS
    y_new = _dot_nt(qbd, _pad_rows(kn_ref[0], blk).astype(BF16)) + bias
    y_pages = [_dot(qbd, kt[0].astype(BF16)) for kt in kt_refs]
    ys = [jnp.where(key_i < qry_t, y_new, MASKED_LOGIT)]
    ys += [jnp.concatenate([y_pages[2 * i], y_pages[2 * i + 1]], axis=1) + bias for i in reversed(range(n_pages // 2))]
    blocks = _sb_blocks(ys, _suffix_ones(blk))
    carry = jnp.zeros((LANES, 1), F32)
    weights = []
    for a, tot in blocks:
        weights.append((a * jnp.exp2(-carry)).astype(BF16))
        carry = carry + tot
    out_new = _dot(weights[0], _pad_rows(vn_ref[0], blk).astype(BF16))
    out_t = None
    for w, i in zip(weights[1:], reversed(range(n_pages // 2))):
        v_t = jnp.concatenate([vt_refs[2 * i][0], vt_refs[2 * i + 1][0]], axis=1).astype(BF16)
        part = _dot_nt(v_t, w)
        out_t = part if out_t is None else out_t + part
    rows = [jnp.sum(jnp.where(head_mask, out_new[t * SB_HEADS:(t + 1) * SB_HEADS, :], 0.0), axis=0, keepdims=True)
            for t in range(t_dec)]
    f_head = lax.broadcasted_iota(jnp.int32, (d, LANES), 0) // hd
    r_idx = lax.broadcasted_iota(jnp.int32, (d, LANES), 1)
    hi, lo = _split_bf16(jnp.where((r_idx % SB_HEADS == f_head) & (r_idx < nrow), out_t, 0.0))
    sel = jnp.where(lax.broadcasted_iota(jnp.int32, (16, LANES), 1) // SB_HEADS
                    == lax.broadcasted_iota(jnp.int32, (16, LANES), 0), 1.0, 0.0).astype(BF16)
    past = _dot_nt(sel, hi) + _dot_nt(sel, lo)
    o_ref[0] = jnp.concatenate(rows, axis=0) + past[:t_dec]


def _feature_major_pages(cache):
    n_pool, page, heads, hd = cache.shape
    return jnp.transpose(cache, (0, 2, 3, 1)).reshape(n_pool, heads * hd, page)


def _sb_sample(q, k_new, v_new, pages_kt, pages_vt, page_table, bias, t_dec, hd):
    n_s, d = q.shape
    n_seq = n_s // t_dec
    page = pages_kt.shape[2]
    n_pages = page_table.shape[1]
    nrow = t_dec * SB_HEADS
    assert page == LANES and nrow <= LANES and n_pages % 2 == 0 and t_dec <= 16
    bias_rows = jnp.broadcast_to(jnp.tile(bias.astype(F32), t_dec)[:, None], (nrow, LANES))
    bias_rows = jnp.concatenate([bias_rows, jnp.zeros((LANES - nrow, LANES), F32)], axis=0)
    seq3 = lambda a: a.reshape(n_seq, t_dec, d)
    seq_spec = pl.BlockSpec((1, t_dec, d), lambda b, pt: (b, 0, 0))

    def page_spec(j):
        return pl.BlockSpec((1, d, page), lambda b, pt: (pt[b * n_pages + j], 0, 0))

    kern = functools.partial(_sb_sample_kernel, t_dec=t_dec, hd=hd, n_pages=n_pages)
    out = pl.pallas_call(
        kern,
        grid_spec=pltpu.PrefetchScalarGridSpec(
            num_scalar_prefetch=1, grid=(n_seq,),
            in_specs=[seq_spec, seq_spec, seq_spec, pl.BlockSpec((LANES, LANES), lambda b, pt: (0, 0))]
                     + [page_spec(j) for j in range(n_pages)] + [page_spec(j) for j in range(n_pages)],
            out_specs=seq_spec),
        out_shape=jax.ShapeDtypeStruct((n_seq, t_dec, d), F32),
        compiler_params=_params("arbitrary"), name="sb_sample",
    )(page_table.reshape(-1), seq3(q), seq3(k_new), seq3(v_new), bias_rows,
      *([pages_kt] * n_pages), *([pages_vt] * n_pages))
    return out.reshape(n_s, d)


def kernel(x_prompt, x_sample, p_prompt, p_sample, state_ret, cache_k, cache_v, page_table, g_norm, w_ff1_gate, w_ff1_up, w_ff1_down, w_ff2_gate, w_ff2_up, w_ff2_down, w_ple_up, w_ple_gate, g_ple, w_ret_in, w_ret_out, g_ret_gn, g_kv, w_kv, w_sb_q, w_sb_out, b_sb, g_final):
    batch, t_prompt, d = x_prompt.shape
    n_dec, t_dec, _ = x_sample.shape
    depth = g_norm.shape[0]
    n_ret = w_ret_in.shape[0]
    dk = d // RET_HEADS
    dv = w_ret_out.shape[1] // RET_HEADS
    hd = d // SB_HEADS
    past_len = page_table.shape[1] * cache_k.shape[1]
    row = lambda a: a.reshape(1, -1)
    bf = lambda a: a.astype(BF16)

    xp = x_prompt.reshape(batch * t_prompt, d)
    xs = x_sample.reshape(n_dec * t_dec, d)
    x_out = [(d, F32)]
    states_p, states_s = [], None
    pages_kt, pages_vt = _feature_major_pages(cache_k), _feature_major_pages(cache_v)
    ff1 = [bf(w_ff1_gate), bf(w_ff1_up), bf(w_ff1_down)]
    ff2 = [bf(w_ff2_gate), bf(w_ff2_up), bf(w_ff2_down)]
    ple = [bf(w_ple_up), bf(w_ple_gate)]
    ret_in, ret_out, sb_q, sb_out = bf(w_ret_in), bf(w_ret_out), bf(w_sb_q), bf(w_sb_out)
    p_p = p_prompt.reshape(depth, batch * t_prompt, -1)
    p_s = p_sample.reshape(depth, n_dec * t_dec, -1)
    for i in range(depth):
        ffn1_consts = [row(g_norm[i, 0])] + [_Layer(w, i) for w in ff1]
        streams = [(xp, xs), (_Layer(p_p, i), _Layer(p_s, i))]
        if i < n_ret:
            (xp, xs), = _tokenwise(_ffn_body, [(xp, xs)], ffn1_consts, x_out, "ffn1")
            g_gn = row(g_ret_gn[i])
            xp, s_p = _retention_prompt(xp, batch, row(g_norm[i, 1]), _Layer(ret_in, i), _Layer(ret_out, i),
                                        g_gn, dk, dv)
            q, k, v, g = _rowwise(functools.partial(_ret_in_proj_body, qk=RET_HEADS * dk, vd=RET_HEADS * dv),
                                  [xs], [row(g_norm[i, 1]), _Layer(ret_in, i)],
                                  [(RET_HEADS * dk, F32), (RET_HEADS * dk, F32),
                                   (RET_HEADS * dv, F32), (RET_HEADS * dv, F32)], "ret_in_proj_sample")
            gated, states_s = _retention_sample(q, k, v, g, g_gn, state_ret, i, states_s, t_dec, past_len, dk, dv)
            xs, = _rowwise(_proj_residual_body, [xs, gated], [_Layer(ret_out, i)], x_out, "ret_out_proj_sample")
            states_p.append(s_p)
            streams[0] = (xp, xs)
            mixer_out = []
        else:
            l = i - n_ret
            (xp, xs), (qp, qs) = _tokenwise(functools.partial(_ffn_q_body, scale=LOG2E * hd ** -0.5), [(xp, xs)],
                                            ffn1_consts + [row(g_norm[i, 1]), _Layer(sb_q, l)],
                                            x_out + [(d, BF16)], "ffn1_q")
            bias2 = b_sb[l].astype(F32) * LOG2E
            o_p = _sb_prompt(qp, kp16, vp16, bias2, batch, hd)
            o_s = _sb_sample(qs.astype(F32), ks32, vs32, pages_kt, pages_vt, page_table, bias2, t_dec, hd)
            streams = [(xp, xs), streams[1], (o_p, bf(o_s))]
            mixer_out = [_Layer(sb_out, l)]
        last = i == depth - 1
        consts = ([row(g_norm[i, 2])] + [_Layer(w, i) for w in ff2] + [row(g_ple[i])] + [_Layer(w, i) for w in ple]
                  + mixer_out + ([row(g_final)] if last else []))
        (xp, xs), = _tokenwise(functools.partial(_ffn_ple_body, mixer_proj=bool(mixer_out), final=last),
                               streams, consts, x_out, "ffn2_ple")
        if i == n_ret - 1:
            kpt, vpt, kp16, vp16 = _kv_prompt(xp, batch, row(g_kv), bf(w_kv))
            ks32, vs32 = _rowwise(_kv_body, [xs], [row(g_kv), bf(w_kv)], [(d, F32), (d, F32)], "kv_proj_sample")
    yp, ys = xp, xs
    heads_last = lambda a: jnp.transpose(a.reshape(batch, SB_HEADS, hd, t_prompt), (0, 3, 1, 2))
    return (yp.reshape(batch, t_prompt, d), ys.reshape(n_dec, t_dec, d),
            jnp.stack(states_p), states_s,
            heads_last(kpt), heads_last(vpt),
            ks32.reshape(n_dec, t_dec, SB_HEADS, hd), vs32.reshape(n_dec, t_dec, SB_HEADS, hd))
```

```python
import functools
import math

import jax
import jax.numpy as jnp
from jax import lax
from jax.experimental import pallas as pl
from jax.experimental.pallas import tpu as pltpu

F32 = jnp.float32
BF16 = jnp.bfloat16

EPS = 1e-6
ROPE_BASE = 10000.0
RET_HEADS = 4
SB_HEADS = 16
LANES = 128
VMEM_LIMIT_BYTES = 58 * 1024 * 1024
MASKED_LOGIT = -1e30
LOG2E = 1.4426950408889634
TOKEN_TILE = 512
RET_CHUNK_ROWS = 512
SB_QUERY_BLOCK = 512
SB_KEY_BLOCK = 256
SB_HEADS_PER_STEP = 4
SB_BLOCKS_PER_VISIT = 4


def _params(*sem, flags=None):
    return pltpu.CompilerParams(dimension_semantics=sem, vmem_limit_bytes=VMEM_LIMIT_BYTES, flags=flags)


def _dot(a, b):
    return jnp.dot(a, b, preferred_element_type=F32)


def _dot_nt(a, b):
    return lax.dot_general(a, b, (((1,), (1,)), ((), ())), preferred_element_type=F32)


def _rmsnorm(x, g):
    y = x * lax.rsqrt(jnp.mean(x * x, axis=-1, keepdims=True) + EPS)
    return y * g


def _silu(x):
    return x * jax.nn.sigmoid(x)


def _softplus2(y):
    neg_abs = lax.bitcast_convert_type(lax.bitcast_convert_type(y, jnp.int32) | jnp.int32(-2 ** 31), F32)
    return jnp.maximum(y, 0.0) + jnp.log(1.0 + jnp.exp2(neg_abs)) * LOG2E


def _split_bf16(x):
    hi = x.astype(BF16)
    lo = (x - hi.astype(F32)).astype(BF16)
    return hi, lo


class _Layer:
    def __init__(self, stack, layer):
        self.stack, self.layer = stack, layer
        self.shape = stack.shape[1:]


def _array(a):
    return a.stack if isinstance(a, _Layer) else a


def _resident(arr):
    nd = len(arr.shape)
    if isinstance(arr, _Layer):
        return pl.BlockSpec((None,) + tuple(arr.shape), lambda *_: (arr.layer,) + (0,) * nd,
                            pipeline_mode=pl.Buffered(1))
    return pl.BlockSpec(arr.shape, lambda *_: (0,) * nd, pipeline_mode=pl.Buffered(1))


def _tokenwise(body, streams, consts, outs, name):
    n_p, n_s = streams[0][0].shape[0], streams[0][1].shape[0]
    tm = min(TOKEN_TILE, n_s)
    assert n_p % tm == 0 and n_s % tm == 0
    p_tiles, s_tiles = n_p // tm, n_s // tm
    ns, nc, no = len(streams), len(consts), len(outs)

    def kern(*refs):
        in_refs, c_refs, o_refs = refs[:2 * ns], refs[2 * ns:2 * ns + nc], refs[2 * ns + nc:]
        is_prompt = pl.program_id(0) < p_tiles
        vals = [jnp.where(is_prompt, in_refs[2 * k][...], in_refs[2 * k + 1][...]) for k in range(ns)]
        res = body(*vals, *c_refs)

        @pl.when(is_prompt)
        def _():
            for k in range(no):
                o_refs[2 * k][...] = res[k].astype(outs[k][1])

        @pl.when(jnp.logical_not(is_prompt))
        def _():
            for k in range(no):
                o_refs[2 * k + 1][...] = res[k].astype(outs[k][1])

    def p_spec(w, layer=None):
        if layer is not None:
            return pl.BlockSpec((None, tm, w), lambda i: (layer, jnp.minimum(i, p_tiles - 1), 0))
        return pl.BlockSpec((tm, w), lambda i: (jnp.minimum(i, p_tiles - 1), 0))

    def s_spec(w, layer=None):
        if layer is not None:
            return pl.BlockSpec((None, tm, w), lambda i: (layer, jnp.maximum(i - p_tiles, 0), 0))
        return pl.BlockSpec((tm, w), lambda i: (jnp.maximum(i - p_tiles, 0), 0))

    in_specs, args = [], []
    for a_p, a_s in streams:
        layer = a_p.layer if isinstance(a_p, _Layer) else None
        in_specs += [p_spec(a_p.shape[1], layer), s_spec(a_s.shape[1], layer)]
        args += [_array(a_p), _array(a_s)]
    in_specs += [_resident(c) for c in consts]
    consts = [_array(c) for c in consts]
    out_specs, out_shape = [], []
    for w, dt in outs:
        out_specs += [p_spec(w), s_spec(w)]
        out_shape += [jax.ShapeDtypeStruct((n_p, w), dt), jax.ShapeDtypeStruct((n_s, w), dt)]
    res = pl.pallas_call(
        kern, grid=(p_tiles + s_tiles,), in_specs=in_specs, out_specs=out_specs,
        out_shape=out_shape, compiler_params=_params("arbitrary"), name=name,
    )(*args, *consts)
    return [(res[2 * k], res[2 * k + 1]) for k in range(no)]


def _rowwise(body, rows, consts, outs, name):
    n = rows[0].shape[0]
    tm = min(TOKEN_TILE, n)
    assert n % tm == 0
    nr, nc, no = len(rows), len(consts), len(outs)

    def kern(*refs):
        res = body(*[r[...] for r in refs[:nr]], *refs[nr:nr + nc])
        for k in range(no):
            refs[nr + nc + k][...] = res[k].astype(outs[k][1])

    in_specs = [pl.BlockSpec((tm, a.shape[1]), lambda i: (i, 0)) for a in rows]
    in_specs += [_resident(c) for c in consts]
    consts = [_array(c) for c in consts]
    return pl.pallas_call(
        kern, grid=(n // tm,), in_specs=in_specs,
        out_specs=[pl.BlockSpec((tm, w), lambda i: (i, 0)) for w, _ in outs],
        out_shape=[jax.ShapeDtypeStruct((n, w), dt) for w, dt in outs],
        compiler_params=_params("arbitrary"), name=name,
    )(*rows, *consts)


def _ffn_body(x, g_ref, wg_ref, wu_ref, wd_ref):
    h = _rmsnorm(x, g_ref[...]).astype(BF16)
    d_ff = wg_ref.shape[1]
    chunk = 256 if d_ff % 256 == 0 else LANES
    acc = None
    for c0 in range(0, d_ff, chunk):
        gate = _dot(h, wg_ref[:, c0:c0 + chunk])
        up = _dot(h, wu_ref[:, c0:c0 + chunk])
        d = _dot((_silu(gate) * up).astype(BF16), wd_ref[c0:c0 + chunk, :])
        acc = d if acc is None else acc + d
    return (x + 0.5 * acc,)


def _ple_body(x, p, g_ref, wup_ref, wgate_ref):
    h = _rmsnorm(x, g_ref[...]).astype(BF16)
    gate = jax.nn.sigmoid(_dot(h, wgate_ref[...]))
    return (x + _dot(p.astype(BF16), wup_ref[...]) * gate,)


def _ffn_q_body(x, g_ref, wg_ref, wu_ref, wd_ref, gq_ref, wq_ref, *, scale):
    (x,) = _ffn_body(x, g_ref, wg_ref, wu_ref, wd_ref)
    return (x, _dot(_rmsnorm(x, gq_ref[...]).astype(BF16), wq_ref[...]) * scale)


def _kv_body(x, g_ref, w_ref):
    h = _rmsnorm(x, g_ref[...]).astype(BF16)
    half = w_ref.shape[1] // 2
    return (_dot(h, w_ref[:, :half]), _dot(h, w_ref[:, half:]))


def _kv_prompt_kernel(x_ref, g_ref, w_ref, kt_ref, vt_ref, k16_ref, v16_ref):
    k, v = _kv_body(x_ref[...], g_ref, w_ref)
    kt = k.T
    kt_ref[0] = kt
    vt_ref[0] = v.T
    k16_ref[0] = kt.astype(BF16)
    v16_ref[...] = v.astype(BF16)


def _kv_prompt(xp, batch, g, w):
    n_p, d = xp.shape
    t = n_p // batch
    tm = min(TOKEN_TILE, t)
    assert t % tm == 0
    nt = t // tm
    row_spec = pl.BlockSpec((tm, d), lambda b, i: (b * nt + i, 0))
    fm_spec = pl.BlockSpec((1, d, tm), lambda b, i: (b, 0, i))
    return pl.pallas_call(
        _kv_prompt_kernel, grid=(batch, nt),
        in_specs=[row_spec, _resident(g), _resident(w)],
        out_specs=[fm_spec, fm_spec, fm_spec, row_spec],
        out_shape=[jax.ShapeDtypeStruct((batch, d, t), F32), jax.ShapeDtypeStruct((batch, d, t), F32),
                   jax.ShapeDtypeStruct((batch, d, t), BF16), jax.ShapeDtypeStruct((n_p, d), BF16)],
        compiler_params=_params("arbitrary", "arbitrary"), name="kv_proj_prompt",
    )(xp, g, w)


def _proj_residual_body(x, a, w_ref):
    return (x + _dot(a.astype(BF16), w_ref[...]),)


def _ffn_ple_body(x, p, *rest, mixer_proj, final):
    rest = list(rest)
    mixed = rest.pop(0) if mixer_proj else None
    ffn_refs, ple_refs, tail = rest[:4], rest[4:7], rest[7:]
    if mixer_proj:
        x = x + _dot(mixed, tail.pop(0)[...])
    (x,) = _ffn_body(x, *ffn_refs)
    (x,) = _ple_body(x, p, *ple_refs)
    return (_rmsnorm(x, tail[0][...]) if final else x,)


def _ret_in_proj_body(x, g_ref, w_ref, *, qk, vd):
    h = _rmsnorm(x, g_ref[...]).astype(BF16)
    return (_dot(h, w_ref[:, :qk]), _dot(h, w_ref[:, qk:2 * qk]),
            _dot(h, w_ref[:, 2 * qk:2 * qk + vd]), _dot(h, w_ref[:, 2 * qk + vd:]))


def _rope_tables(pos, half):
    inv_freq = ROPE_BASE ** (-jnp.arange(half, dtype=F32) / half)
    ang = pos.astype(F32)[:, None] * inv_freq[None, :]
    return jnp.cos(ang), jnp.sin(ang)


def _rope(x, cos, sin):
    half = x.shape[-1] // 2
    x1, x2 = x[:, :half], x[:, half:]
    return jnp.concatenate([x1 * cos - x2 * sin, x2 * cos + x1 * sin], axis=-1)


def _log_decay(head):
    return math.log(1.0 - 2.0 ** (-5.0 - head))


def _group_norm_gate(o, g, ggn):
    mu = jnp.mean(o, axis=-1, keepdims=True)
    var = jnp.mean(jnp.square(o - mu), axis=-1, keepdims=True)
    on = (o - mu) * lax.rsqrt(var + EPS) * ggn
    return _silu(g) * on


def _ret_prompt_kernel(x_ref, gn_ref, win_ref, wout_ref, ggn_ref, cos_ref, sin_ref,
                       o_ref, s_ref, *, rows, dk, dv):
    qk, vd = RET_HEADS * dk, RET_HEADS * dv

    @pl.when(pl.program_id(1) == 0)
    def _():
        s_ref[...] = jnp.zeros_like(s_ref)

    x = x_ref[...]
    h = _rmsnorm(x, gn_ref[...]).astype(BF16)
    cos, sin = cos_ref[...], sin_ref[...]
    diff = (lax.broadcasted_iota(jnp.int32, (rows, rows), 0)
            - lax.broadcasted_iota(jnp.int32, (rows, rows), 1)).astype(F32)
    ri = lax.broadcasted_iota(jnp.int32, (rows, 1), 0).astype(F32)
    acc = None
    for hh in range(RET_HEADS):
        lg = _log_decay(hh)
        q = _rope(_dot(h, win_ref[:, hh * dk:(hh + 1) * dk]), cos, sin)
        k = _rope(_dot(h, win_ref[:, qk + hh * dk:qk + (hh + 1) * dk]), cos, sin) * (dk ** -0.5)
        v = _dot(h, win_ref[:, 2 * qk + hh * dv:2 * qk + (hh + 1) * dv]).astype(BF16)
        g = _dot(h, win_ref[:, 2 * qk + vd + hh * dv:2 * qk + vd + (hh + 1) * dv])
        decay = jnp.where(diff >= 0, jnp.exp(jnp.maximum(diff, 0.0) * lg), 0.0)
        s = _dot_nt(q.astype(BF16), k.astype(BF16)) * decay
        state = s_ref[0, hh]
        q_dec = (q * jnp.exp((ri + 1.0) * lg)).astype(BF16)
        o = _dot(s.astype(BF16), v) + _dot(q_dec, state.astype(BF16))
        k_dec = k * jnp.exp((rows - 1.0 - ri) * lg)
        s_ref[0, hh] = math.exp(rows * lg) * state + _dot(k_dec.T.astype(BF16), v)
        gated = _group_norm_gate(o, g, ggn_ref[:, hh * dv:(hh + 1) * dv]).astype(BF16)
        d = _dot(gated, wout_ref[hh * dv:(hh + 1) * dv, :])
        acc = d if acc is None else acc + d
    o_ref[...] = x + acc


def _retention_prompt(xp, batch, g, w_in, w_out, g_gn, dk, dv):
    n_p, d = xp.shape
    t = n_p // batch
    rows = min(RET_CHUNK_ROWS, t)
    assert t % rows == 0
    nch = t // rows
    cos, sin = _rope_tables(jnp.arange(t, dtype=jnp.int32), dk // 2)
    kern = functools.partial(_ret_prompt_kernel, rows=rows, dk=dk, dv=dv)
    return pl.pallas_call(
        kern, grid=(batch, nch),
        in_specs=[pl.BlockSpec((rows, d), lambda b, c: (b * nch + c, 0)),
                  _resident(g), _resident(w_in), _resident(w_out), _resident(g_gn),
                  pl.BlockSpec((rows, dk // 2), lambda b, c: (c, 0)),
                  pl.BlockSpec((rows, dk // 2), lambda b, c: (c, 0))],
        out_specs=[pl.BlockSpec((rows, d), lambda b, c: (b * nch + c, 0)),
                   pl.BlockSpec((1, RET_HEADS, dk, dv), lambda b, c: (b, 0, 0, 0))],
        out_shape=[jax.ShapeDtypeStruct((n_p, d), F32),
                   jax.ShapeDtypeStruct((batch, RET_HEADS, dk, dv), F32)],
        compiler_params=_params("arbitrary", "arbitrary"), name="retention_prompt",
    )(xp, g, _array(w_in), _array(w_out), g_gn, cos, sin)


def _pad_rows(x, rows):
    return jnp.concatenate([x, jnp.zeros((rows - x.shape[0], x.shape[1]), x.dtype)], axis=0)


def _ret_sample_kernel(q_ref, k_ref, v_ref, g_ref, ggn_ref, cos_ref, sin_ref, s_ref, *rest,
                       t_dec, dk, dv, n_fill):
    o_ref, so_ref = rest[-2], rest[-1]
    nrow = 2 * t_dec
    cos, sin = cos_ref[...], sin_ref[...]
    r_i = lax.broadcasted_iota(jnp.int32, (nrow, LANES), 0)
    c_i = lax.broadcasted_iota(jnp.int32, (nrow, LANES), 1)
    same_seq = (r_i // t_dec) == (c_i // t_dec)
    diff = ((r_i % t_dec) - (c_i % t_dec)).astype(F32)
    valid = same_seq & (diff >= 0) & (c_i < nrow)
    t_row = (lax.broadcasted_iota(jnp.int32, (nrow, 1), 0) % t_dec).astype(F32)
    seq_row = lax.broadcasted_iota(jnp.int32, (nrow, 1), 0) // t_dec
    seq_lane = lax.broadcasted_iota(jnp.int32, (1, LANES), 1) // t_dec
    for fill in range(n_fill):
        so_ref[1 + fill] = jnp.zeros(so_ref.shape[1:], so_ref.dtype)
    outs = []
    for hh in range(RET_HEADS):
        lg = _log_decay(hh)
        q = _rope(q_ref[0, :, hh * dk:(hh + 1) * dk], cos, sin)
        k = _rope(k_ref[0, :, hh * dk:(hh + 1) * dk], cos, sin) * (dk ** -0.5)
        v = _pad_rows(v_ref[0, :, hh * dv:(hh + 1) * dv], LANES).astype(BF16)
        g = g_ref[0, :, hh * dv:(hh + 1) * dv]
        decay = jnp.where(valid, jnp.exp(jnp.maximum(diff, 0.0) * lg), 0.0)
        q16 = _pad_rows(q, 16)
        s = _dot_nt(q16.astype(BF16), _pad_rows(k, LANES).astype(BF16))[:nrow] * decay
        o = _dot(_pad_rows(s, 16).astype(BF16), v)[:nrow]
        q_dec = _pad_rows(q * jnp.exp((t_row + 1.0) * lg), 16).astype(BF16)
        k_dec_t = _pad_rows(k * jnp.exp((t_dec - 1.0 - t_row) * lg), LANES).T
        for sq in range(2):
            state = s_ref[0, sq, hh]
            o_state = _dot(q_dec, state.astype(BF16))[:nrow]
            o = o + jnp.where(seq_row == sq, o_state, 0.0)
            k_sq = jnp.where(seq_lane == sq, k_dec_t, 0.0).astype(BF16)
            so_ref[0, sq, hh] = math.exp(t_dec * lg) * state + _dot(k_sq, v)
        outs.append(_group_norm_gate(o, g, ggn_ref[:, hh * dv:(hh + 1) * dv]))
    o_ref[0] = jnp.concatenate(outs, axis=-1)


def _retention_sample(q, k, v, g, g_gn, state_all, layer, prev_states, t_dec, past_len, dk, dv):
    n_s = q.shape[0]
    n_seq = n_s // t_dec
    n_layers = state_all.shape[0]
    assert n_seq % 2 == 0
    nrow = 2 * t_dec
    pos = past_len + (jnp.arange(nrow, dtype=jnp.int32) % t_dec)
    cos, sin = _rope_tables(pos, dk // 2)
    grp = lambda a: a.reshape(n_seq // 2, nrow, a.shape[1])
    qk, vd = RET_HEADS * dk, RET_HEADS * dv
    first = prev_states is None
    n_fill = n_layers - 1 if first else 0
    kern = functools.partial(_ret_sample_kernel, t_dec=t_dec, dk=dk, dv=dv, n_fill=n_fill)
    row_spec = lambda w: pl.BlockSpec((1, nrow, w), lambda i: (i, 0, 0))
    st_in = pl.BlockSpec((1, 2, RET_HEADS, dk, dv), lambda i: (layer, i, 0, 0, 0))
    st_out = pl.BlockSpec((1 + n_fill, 2, RET_HEADS, dk, dv), lambda i: (0 if first else layer, i, 0, 0, 0))
    in_specs = [row_spec(qk), row_spec(qk), row_spec(vd), row_spec(vd), _resident(g_gn),
                _resident(cos), _resident(sin), st_in]
    args = [grp(q), grp(k), grp(v), grp(g), g_gn, cos, sin, state_all]
    aliases = {}
    if not first:
        in_specs.append(pl.BlockSpec(memory_space=pl.ANY))
        args.append(prev_states)
        aliases = {len(args) - 1: 1}
    gated, s_new = pl.pallas_call(
        kern, grid=(n_seq // 2,), in_specs=in_specs,
        out_specs=[row_spec(vd), st_out],
        out_shape=[jax.ShapeDtypeStruct((n_seq // 2, nrow, vd), F32),
                   jax.ShapeDtypeStruct(state_all.shape, F32)],
        input_output_aliases=aliases,
        compiler_params=_params("arbitrary"), name="retention_sample",
    )(*args)
    return gated.reshape(n_s, vd), s_new


def _suffix_ones(n, strict=False):
    r = lax.broadcasted_iota(jnp.int32, (n, n), 0)
    c = lax.broadcasted_iota(jnp.int32, (n, n), 1)
    return jnp.where((r > c) if strict else (r >= c), 1.0, 0.0).astype(BF16)


def _sb_blocks(ys, ones):
    parts = [_split_bf16(_softplus2(y)) for y in ys]
    suffixes = [_dot(hi, ones) + _dot(lo, ones) for hi, lo in parts]
    return [(jnp.exp2(y - sfx), sfx[:, :1]) for y, sfx in zip(ys, suffixes)]


def _sb_blocks_skewed(n, logits, strict_ones, use):
    def own_and_later(y):
        sp = _softplus2(y)
        return y - sp, _dot(sp.astype(BF16), strict_ones), sp[:, :1]

    parts, results = {}, []
    for step in range(n + 2):
        if step < n:
            parts[step] = logits(step)
        if 1 <= step <= n:
            parts[step - 1] = own_and_later(parts[step - 1])
        if step >= 2:
            own, later, first = parts.pop(step - 2)
            results.append(use(step - 2, jnp.exp2(own - later), later[:, :1] + first))
    return results


def _sb_prompt_kernel(bias_ref, q_ref, k_ref, v_ref, o_ref, qh_ref, ones_ref, bm_ref, acc_ref, carry_ref,
                      *, tq, tk, hd, n_pairs):
    grp, qi = pl.program_id(1), pl.program_id(2)
    ratio = tq // tk
    lo_lanes = lax.broadcasted_iota(jnp.int32, (1, LANES), 1) < hd
    row = lax.broadcasted_iota(jnp.int32, (tq, tk), 0)
    col = lax.broadcasted_iota(jnp.int32, (tq, tk), 1)
    for p in range(n_pairs):
        qp = q_ref[:, p * LANES:(p + 1) * LANES]
        zero = jnp.zeros_like(qp)
        qh_ref[2 * p] = jnp.where(lo_lanes, qp, zero)
        qh_ref[2 * p + 1] = jnp.where(lo_lanes, zero, qp)

    @pl.when(qi == 0)
    def _():
        for h in range(2 * n_pairs):
            bm_ref[h] = jnp.where(col < row, bias_ref[grp * 2 * n_pairs + h], MASKED_LOGIT)
        ones_ref[...] = _suffix_ones(tk, strict=True)

    acc_ref[...] = jnp.zeros_like(acc_ref)
    carry_ref[...] = jnp.zeros_like(carry_ref)
    n_heads = 2 * n_pairs
    past = ratio * qi

    def visit(top, count, overlap):
        ones = ones_ref[...]
        starts = [pl.multiple_of((top - u) * tk, tk) for u in range(count)]
        row0 = [(ratio - 1 - u) * tk if overlap else 0 for u in range(count)]
        biases = [bias_ref[grp * n_heads + h] for h in range(n_heads)]

        def logits(i):
            u, h = divmod(i, n_heads)
            kb = k_ref[0, (h // 2) * LANES:(h // 2 + 1) * LANES, pl.ds(starts[u], tk)]
            s = _dot(qh_ref[h, row0[u]:, :], kb)
            return s + (bm_ref[h, :tq - row0[u], :] if overlap else biases[h])

        def weigh(i, a, tot):
            u, h = divmod(i, n_heads)
            vb = v_ref[pl.ds(starts[u], tk), (h // 2) * LANES:(h // 2 + 1) * LANES]
            vzero = jnp.zeros_like(vb)
            vh = jnp.where(lo_lanes, vb, vzero) if h % 2 == 0 else jnp.where(lo_lanes, vzero, vb)
            return _dot(a.astype(BF16), vh), tot

        blocks = _sb_blocks_skewed(count * n_heads, logits, ones, weigh)
        for p in range(n_pairs):
            lanes = slice(p * LANES, (p + 1) * LANES)
            if overlap:
                for u in range(count):
                    (d0, t0), (d1, t1) = blocks[u * n_heads + 2 * p], blocks[u * n_heads + 2 * p + 1]
                    c = carry_ref[p, row0[u]:, :]
                    acc_ref[row0[u]:, lanes] += jnp.exp2(-c) * (d0 + d1)
                    carry_ref[p, row0[u]:, :] = c + jnp.where(lo_lanes, t0, t1)
            else:
                c = carry_ref[p]
                acc = acc_ref[:, lanes]
                for u in range(count):
                    (d0, t0), (d1, t1) = blocks[u * n_heads + 2 * p], blocks[u * n_heads + 2 * p + 1]
                    acc = acc + jnp.exp2(-c) * (d0 + d1)
                    c = c + jnp.where(lo_lanes, t0, t1)
                acc_ref[:, lanes] = acc
                carry_ref[p] = c

    visit(past + ratio - 1, ratio, True)
    big = SB_BLOCKS_PER_VISIT
    rem = lax.rem(past, big)
    if big > ratio:
        @pl.when(rem != 0)
        def _():
            visit(past - 1, ratio, False)

    def body(it, _):
        visit(past - 1 - rem - it * big, big, False)
        return 0

    lax.fori_loop(0, past // big, body, 0)
    o_ref[...] = acc_ref[...].astype(o_ref.dtype)


def _sb_prompt(q, k, v, bias, batch, hd):
    n_p, d = q.shape
    t = n_p // batch
    tq, tk = min(SB_QUERY_BLOCK, t), min(SB_KEY_BLOCK, t)
    n_pairs = SB_HEADS_PER_STEP // 2
    width = n_pairs * LANES
    assert t % tq == 0 and tq % tk == 0 and 2 * hd == LANES and d % width == 0
    assert SB_BLOCKS_PER_VISIT % (tq // tk) == 0 and SB_BLOCKS_PER_VISIT <= 2 * (tq // tk)
    nq = t // tq
    kern = functools.partial(_sb_prompt_kernel, tq=tq, tk=tk, hd=hd, n_pairs=n_pairs)
    return pl.pallas_call(
        kern, grid=(batch, d // width, nq),
        in_specs=[pl.BlockSpec(memory_space=pltpu.SMEM),
                  pl.BlockSpec((tq, width), lambda b, g, i: (b * nq + i, g)),
                  pl.BlockSpec((1, width, t), lambda b, g, i: (b, g, 0)),
                  pl.BlockSpec((t, width), lambda b, g, i: (b, g))],
        out_specs=pl.BlockSpec((tq, width), lambda b, g, i: (b * nq + i, g)),
        out_shape=jax.ShapeDtypeStruct((n_p, d), BF16),
        scratch_shapes=[pltpu.VMEM((2 * n_pairs, tq, LANES), BF16), pltpu.VMEM((tk, tk), BF16),
                        pltpu.VMEM((2 * n_pairs, tq, tk), F32), pltpu.VMEM((tq, width), F32),
                        pltpu.VMEM((n_pairs, tq, LANES), F32)],
        compiler_params=_params("arbitrary", "arbitrary", "arbitrary"), name="sb_prompt",
    )(bias, q, k, v)


def _sb_sample_kernel(pt_ref, q_ref, kn_ref, vn_ref, bias_ref, *refs, t_dec, hd, n_pages):
    del pt_ref
    kt_refs, vt_refs, o_ref = refs[:n_pages], refs[n_pages:2 * n_pages], refs[2 * n_pages]
    page = kt_refs[0].shape[2]
    nrow = t_dec * SB_HEADS
    d = SB_HEADS * hd
    blk = 2 * page
    head_mask = ((lax.broadcasted_iota(jnp.int32, (SB_HEADS, d), 1) // hd)
                 == lax.broadcasted_iota(jnp.int32, (SB_HEADS, d), 0))
    bias = bias_ref[:, :1]
    qv = q_ref[0]
    qbd = jnp.concatenate([jnp.where(head_mask, qv[t:t + 1, :], 0.0) for t in range(t_dec)]
                          + [jnp.zeros((LANES - nrow, d), F32)], axis=0).astype(BF16)
    key_i = lax.broadcasted_iota(jnp.int32, (LANES, blk), 1)
    qry_t = lax.broadcasted_iota(jnp.int32, (LANES, blk), 0) // SB_HEADS
    y_new = _dot_nt(qbd, _pad_rows(kn_ref[0], blk).astype(BF16)) + bias
    y_pages = [_dot(qbd, kt[0].astype(BF16)) for kt in kt_refs]
    ys = [jnp.where(key_i < qry_t, y_new, MASKED_LOGIT)]
    ys += [jnp.concatenate([y_pages[2 * i], y_pages[2 * i + 1]], axis=1) + bias for i in reversed(range(n_pages // 2))]
    blocks = _sb_blocks(ys, _suffix_ones(blk))
    carry = jnp.zeros((LANES, 1), F32)
    weights = []
    for a, tot in blocks:
        weights.append((a * jnp.exp2(-carry)).astype(BF16))
        carry = carry + tot
    out_new = _dot(weights[0], _pad_rows(vn_ref[0], blk).astype(BF16))
    out_t = None
    for w, i in zip(weights[1:], reversed(range(n_pages // 2))):
        v_t = jnp.concatenate([vt_refs[2 * i][0], vt_refs[2 * i + 1][0]], axis=1).astype(BF16)
        part = _dot_nt(v_t, w)
        out_t = part if out_t is None else out_t + part
    rows = [jnp.sum(jnp.where(head_mask, out_new[t * SB_HEADS:(t + 1) * SB_HEADS, :], 0.0), axis=0, keepdims=True)
            for t in range(t_dec)]
    f_head = lax.broadcasted_iota(jnp.int32, (d, LANES), 0) // hd
    r_idx = lax.broadcasted_iota(jnp.int32, (d, LANES), 1)
    hi, lo = _split_bf16(jnp.where((r_idx % SB_HEADS == f_head) & (r_idx < nrow), out_t, 0.0))
    sel = jnp.where(lax.broadcasted_iota(jnp.int32, (16, LANES), 1) // SB_HEADS
                    == lax.broadcasted_iota(jnp.int32, (16, LANES), 0), 1.0, 0.0).astype(BF16)
    past = _dot_nt(sel, hi) + _dot_nt(sel, lo)
    o_ref[0] = jnp.concatenate(rows, axis=0) + past[:t_dec]


def _feature_major_pages(cache):
    n_pool, page, heads, hd = cache.shape
    return jnp.transpose(cache, (0, 2, 3, 1)).reshape(n_pool, heads * hd, page)


def _sb_sample(q, k_new, v_new, pages_kt, pages_vt, page_table, bias, t_dec, hd):
    n_s, d = q.shape
    n_seq = n_s // t_dec
    page = pages_kt.shape[2]
    n_pages = page_table.shape[1]
    nrow = t_dec * SB_HEADS
    assert page == LANES and nrow <= LANES and n_pages % 2 == 0 and t_dec <= 16
    bias_rows = jnp.broadcast_to(jnp.tile(bias.astype(F32), t_dec)[:, None], (nrow, LANES))
    bias_rows = jnp.concatenate([bias_rows, jnp.zeros((LANES - nrow, LANES), F32)], axis=0)
    seq3 = lambda a: a.reshape(n_seq, t_dec, d)
    seq_spec = pl.BlockSpec((1, t_dec, d), lambda b, pt: (b, 0, 0))

    def page_spec(j):
        return pl.BlockSpec((1, d, page), lambda b, pt: (pt[b * n_pages + j], 0, 0))

    kern = functools.partial(_sb_sample_kernel, t_dec=t_dec, hd=hd, n_pages=n_pages)
    out = pl.pallas_call(
        kern,
        grid_spec=pltpu.PrefetchScalarGridSpec(
            num_scalar_prefetch=1, grid=(n_seq,),
            in_specs=[seq_spec, seq_spec, seq_spec, pl.BlockSpec((LANES, LANES), lambda b, pt: (0, 0))]
                     + [page_spec(j) for j in range(n_pages)] + [page_spec(j) for j in range(n_pages)],
            out_specs=seq_spec),
        out_shape=jax.ShapeDtypeStruct((n_seq, t_dec, d), F32),
        compiler_params=_params("arbitrary"), name="sb_sample",
    )(page_table.reshape(-1), seq3(q), seq3(k_new), seq3(v_new), bias_rows,
      *([pages_kt] * n_pages), *([pages_vt] * n_pages))
    return out.reshape(n_s, d)


def kernel(x_prompt, x_sample, p_prompt, p_sample, state_ret, cache_k, cache_v, page_table, g_norm, w_ff1_gate, w_ff1_up, w_ff1_down, w_ff2_gate, w_ff2_up, w_ff2_down, w_ple_up, w_ple_gate, g_ple, w_ret_in, w_ret_out, g_ret_gn, g_kv, w_kv, w_sb_q, w_sb_out, b_sb, g_final):
    batch, t_prompt, d = x_prompt.shape
    n_dec, t_dec, _ = x_sample.shape
    depth = g_norm.shape[0]
    n_ret = w_ret_in.shape[0]
    dk = d // RET_HEADS
    dv = w_ret_out.shape[1] // RET_HEADS
    hd = d // SB_HEADS
    past_len = page_table.shape[1] * cache_k.shape[1]
    row = lambda a: a.reshape(1, -1)
    bf = lambda a: a.astype(BF16)

    xp = x_prompt.reshape(batch * t_prompt, d)
    xs = x_sample.reshape(n_dec * t_dec, d)
    x_out = [(d, F32)]
    states_p, states_s = [], None
    pages_kt, pages_vt = _feature_major_pages(cache_k), _feature_major_pages(cache_v)
    ff1 = [bf(w_ff1_gate), bf(w_ff1_up), bf(w_ff1_down)]
    ff2 = [bf(w_ff2_gate), bf(w_ff2_up), bf(w_ff2_down)]
    ple = [bf(w_ple_up), bf(w_ple_gate)]
    ret_in, ret_out, sb_q, sb_out = bf(w_ret_in), bf(w_ret_out), bf(w_sb_q), bf(w_sb_out)
    p_p = p_prompt.reshape(depth, batch * t_prompt, -1)
    p_s = p_sample.reshape(depth, n_dec * t_dec, -1)
    for i in range(depth):
        ffn1_consts = [row(g_norm[i, 0])] + [_Layer(w, i) for w in ff1]
        streams = [(xp, xs), (_Layer(p_p, i), _Layer(p_s, i))]
        if i < n_ret:
            (xp, xs), = _tokenwise(_ffn_body, [(xp, xs)], ffn1_consts, x_out, "ffn1")
            g_gn = row(g_ret_gn[i])
            xp, s_p = _retention_prompt(xp, batch, row(g_norm[i, 1]), _Layer(ret_in, i), _Layer(ret_out, i),
                                        g_gn, dk, dv)
            q, k, v, g = _rowwise(functools.partial(_ret_in_proj_body, qk=RET_HEADS * dk, vd=RET_HEADS * dv),
                                  [xs], [row(g_norm[i, 1]), _Layer(ret_in, i)],
                                  [(RET_HEADS * dk, F32), (RET_HEADS * dk, F32),
                                   (RET_HEADS * dv, F32), (RET_HEADS * dv, F32)], "ret_in_proj_sample")
            gated, states_s = _retention_sample(q, k, v, g, g_gn, state_ret, i, states_s, t_dec, past_len, dk, dv)
            xs, = _rowwise(_proj_residual_body, [xs, gated], [_Layer(ret_out, i)], x_out, "ret_out_proj_sample")
            states_p.append(s_p)
            streams[0] = (xp, xs)
            mixer_out = []
        else:
            l = i - n_ret
            (xp, xs), (qp, qs) = _tokenwise(functools.partial(_ffn_q_body, scale=LOG2E * hd ** -0.5), [(xp, xs)],
                                            ffn1_consts + [row(g_norm[i, 1]), _Layer(sb_q, l)],
                                            x_out + [(d, BF16)], "ffn1_q")
            bias2 = b_sb[l].astype(F32) * LOG2E
            o_p = _sb_prompt(qp, kp16, vp16, bias2, batch, hd)
            o_s = _sb_sample(qs.astype(F32), ks32, vs32, pages_kt, pages_vt, page_table, bias2, t_dec, hd)
            streams = [(xp, xs), streams[1], (o_p, bf(o_s))]
            mixer_out = [_Layer(sb_out, l)]
        last = i == depth - 1
        consts = ([row(g_norm[i, 2])] + [_Layer(w, i) for w in ff2] + [row(g_ple[i])] + [_Layer(w, i) for w in ple]
                  + mixer_out + ([row(g_final)] if last else []))
        (xp, xs), = _tokenwise(functools.partial(_ffn_ple_body, mixer_proj=bool(mixer_out), final=last),
                               streams, consts, x_out, "ffn2_ple")
        if i == n_ret - 1:
            kpt, vpt, kp16, vp16 = _kv_prompt(xp, batch, row(g_kv), bf(w_kv))
            ks32, vs32 = _rowwise(_kv_body, [xs], [row(g_kv), bf(w_kv)], [(d, F32), (d, F32)], "kv_proj_sample")
    yp, ys = xp, xs
    heads_last = lambda a: jnp.transpose(a.reshape(batch, SB_HEADS, hd, t_prompt), (0, 3, 1, 2))
    return (yp.reshape(batch, t_prompt, d), ys.reshape(n_dec, t_dec, d),
            jnp.stack(states_p), states_s,
            heads_last(kpt), heads_last(vpt),
            ks32.reshape(n_dec, t_dec, SB_HEADS, hd), vs32.reshape(n_dec, t_dec, SB_HEADS, hd))
```
